```python
import jax, jax.numpy as jnp
from jax import lax
import numpy as np

D_MODEL = 2048
BATCH = 2
SEQ = 4096
DEPTH = 4
DEC_BATCH = 8
DEC_SEQ = 4
PAST_LEN = 16384
PAGE_SIZE = 128

N_MIXERS = 2
N_MOBA_LAYERS = (DEPTH + 1) // 2
N_CONV_LAYERS = DEPTH // 2
N_HEADS = 16
HEAD_DIM = D_MODEL // N_HEADS
ROPE_THETA = 10000.0
MOBA_BLOCK = 256
MOBA_TOPK = 3
Q_CHUNK = 32
CONV_WIDTH = 3
N_EXPERTS = 64
TOP_K = 8
N_GROUPS = 8
TOPK_GROUPS = 4
D_EXPERT = 512
D_SHARED = 512
ROUTED_SCALE = 2.5
MOE_MAX_BLOCK = 128
DEEPNORM_ALPHA = (2 * DEPTH) ** 0.25
DEEPNORM_BETA = (8 * DEPTH) ** -0.25
ADA_SCALE = 0.5
LN_EPS = 1e-5

kernel_name = "moba_shortconv_moe_hybrid_step"


def layer_norm(x, g, b):
    xf = x.astype(jnp.float32)
    mu = xf.mean(-1, keepdims=True)
    var = jnp.square(xf - mu).mean(-1, keepdims=True)
    return ((xf - mu) * lax.rsqrt(var + LN_EPS) * g + b).astype(x.dtype)


def ada_modulation(c, w, b):
    m = jax.nn.silu(c) @ w + b
    shift, scale, gate = jnp.split(m, 3, axis=-1)
    return shift[:, None], scale[:, None], gate[:, None]


def post_norm_residual(x, y, gate, g, b):
    return layer_norm(DEEPNORM_ALPHA * x + gate * y, g, b)


def rope(x, pos):
    half = HEAD_DIM // 2
    inv_freq = ROPE_THETA ** (-jnp.arange(half, dtype=jnp.float32) / half)
    ang = pos.astype(jnp.float32)[:, None] * inv_freq[None, :]
    cos = jnp.cos(ang)[None, :, None, :]
    sin = jnp.sin(ang)[None, :, None, :]
    xf = x.astype(jnp.float32)
    x1, x2 = xf[..., :half], xf[..., half:]
    return jnp.concatenate([x1 * cos - x2 * sin, x2 * cos + x1 * sin], axis=-1).astype(x.dtype)


def qkv_project(h, w_qkv, pos):
    B, L, _ = h.shape
    qkv = (h @ w_qkv).reshape(B, L, 3, N_HEADS, HEAD_DIM)
    return rope(qkv[:, :, 0], pos), rope(qkv[:, :, 1], pos), qkv[:, :, 2]


def moba_select(q, k_mean, q_blk, n_sel):
    s = jnp.einsum('blhd,bnhd->blhn', q, k_mean, preferred_element_type=jnp.float32)
    cand = jnp.arange(k_mean.shape[1])[None, :] < q_blk[:, None]
    s = jnp.where(cand[None, :, None, :], s, -jnp.inf)
    _, idx = lax.top_k(s, n_sel)
    valid = idx < q_blk[None, :, None, None]
    return idx, valid


def moba_attend(q, k_sel, v_sel, sel_valid, k_own, v_own, own_mask):
    B, L, H, _ = q.shape
    qs = q * (HEAD_DIM ** -0.5)
    s_own = jnp.einsum('blhd,bkhd->blhk', qs, k_own, preferred_element_type=jnp.float32)
    s_own = jnp.where(own_mask[None, :, None, :], s_own, -jnp.inf)
    if k_sel is None:
        p = jax.nn.softmax(s_own, axis=-1).astype(v_own.dtype)
        return jnp.einsum('blhk,bkhd->blhd', p, v_own)
    n = k_sel.shape[3]
    s_sel = jnp.einsum('blhd,blhnkd->blhnk', qs, k_sel, preferred_element_type=jnp.float32)
    s_sel = jnp.where(sel_valid[..., None], s_sel, -jnp.inf).reshape(B, L, H, n * MOBA_BLOCK)
    p = jax.nn.softmax(jnp.concatenate([s_sel, s_own], axis=-1), axis=-1)
    p_sel = p[..., :n * MOBA_BLOCK].reshape(B, L, H, n, MOBA_BLOCK).astype(v_sel.dtype)
    p_own = p[..., n * MOBA_BLOCK:].astype(v_own.dtype)
    return (jnp.einsum('blhnk,blhnkd->blhd', p_sel, v_sel)
            + jnp.einsum('blhk,bkhd->blhd', p_own, v_own))


def moba_prompt(h, w_qkv, w_o):
    B, S, _ = h.shape
    pos = jnp.arange(S)
    q, k, v = qkv_project(h, w_qkv, pos)
    nb = -(-S // MOBA_BLOCK)
    pad = nb * MOBA_BLOCK - S
    kb = jnp.pad(k, ((0, 0), (0, pad), (0, 0), (0, 0))).reshape(B, nb, MOBA_BLOCK, N_HEADS, HEAD_DIM)
    vb = jnp.pad(v, ((0, 0), (0, pad), (0, 0), (0, 0))).reshape(B, nb, MOBA_BLOCK, N_HEADS, HEAD_DIM)
    k_mean = kb.astype(jnp.float32).mean(axis=2)
    n_sel = min(MOBA_TOPK, nb - 1)
    b_idx = jnp.arange(B)[:, None, None, None]
    h_idx = jnp.arange(N_HEADS)[None, None, :, None]
    n_chunks = S // Q_CHUNK
    q_chunks = q.reshape(B, n_chunks, Q_CHUNK, N_HEADS, HEAD_DIM).swapaxes(0, 1)

    def one_chunk(args):
        qc, ci = args
        q_pos = ci * Q_CHUNK + jnp.arange(Q_CHUNK)
        own = (ci * Q_CHUNK) // MOBA_BLOCK
        k_own = lax.dynamic_index_in_dim(kb, own, axis=1, keepdims=False)
        v_own = lax.dynamic_index_in_dim(vb, own, axis=1, keepdims=False)
        own_mask = (own * MOBA_BLOCK + jnp.arange(MOBA_BLOCK))[None, :] <= q_pos[:, None]
        if n_sel > 0:
            idx, valid = moba_select(qc, k_mean, q_pos // MOBA_BLOCK, n_sel)
            k_sel = kb[b_idx, idx, :, h_idx, :]
            v_sel = vb[b_idx, idx, :, h_idx, :]
        else:
            k_sel, v_sel, valid = None, None, None
        return moba_attend(qc, k_sel, v_sel, valid, k_own, v_own, own_mask)

    o = lax.map(one_chunk, (q_chunks, jnp.arange(n_chunks)))
    o = o.swapaxes(0, 1).reshape(B, S, N_HEADS * HEAD_DIM)
    return o @ w_o, k, v


def gather_paged_blocks(cache, li, page_table, idx):
    B = idx.shape[0]
    ppb = MOBA_BLOCK // PAGE_SIZE
    logical = idx[..., None] * ppb + jnp.arange(ppb)
    phys = page_table[jnp.arange(B)[:, None, None, None, None], logical]
    h_idx = jnp.arange(N_HEADS)[None, None, :, None, None]
    rows = cache[phys, li, :, h_idx, :]
    return rows.reshape(idx.shape + (MOBA_BLOCK, HEAD_DIM))


def moba_sample(h, cache_k, cache_v, page_table, li, w_qkv, w_o):
    B, L, _ = h.shape
    pos = PAST_LEN + jnp.arange(L)
    q, k, v = qkv_project(h, w_qkv, pos)
    past = page_table.shape[1] * PAGE_SIZE
    nc = past // MOBA_BLOCK
    own_start = nc * MOBA_BLOCK
    k_past = cache_k[page_table, li].reshape(B, past, N_HEADS, HEAD_DIM)
    k_mean = k_past[:, :own_start].astype(jnp.float32).reshape(B, nc, MOBA_BLOCK, N_HEADS, HEAD_DIM).mean(axis=2)
    rem_pages = page_table[:, own_start // PAGE_SIZE:]
    v_rem = cache_v[rem_pages, li].reshape(B, past - own_start, N_HEADS, HEAD_DIM)
    k_own = jnp.concatenate([k_past[:, own_start:], k], axis=1)
    v_own = jnp.concatenate([v_rem, v], axis=1)
    key_pos = own_start + jnp.arange(k_own.shape[1])
    own_mask = (key_pos[None, :] <= pos[:, None]) & (key_pos[None, :] // MOBA_BLOCK == pos[:, None] // MOBA_BLOCK)
    n_sel = min(MOBA_TOPK, nc)
    if n_sel > 0:
        idx, valid = moba_select(q, k_mean, pos // MOBA_BLOCK, n_sel)
        k_sel = gather_paged_blocks(cache_k, li, page_table, idx)
        v_sel = gather_paged_blocks(cache_v, li, page_table, idx)
    else:
        k_sel, v_sel, valid = None, None, None
    o = moba_attend(q, k_sel, v_sel, valid, k_own, v_own, own_mask)
    return o.reshape(B, L, N_HEADS * HEAD_DIM) @ w_o, k, v


def short_conv(h, hist, w_in, w_conv, w_out):
    b_gate, c_gate, val = jnp.split(h @ w_in, 3, axis=-1)
    u = c_gate * val
    u_ext = jnp.concatenate([hist.astype(u.dtype), u], axis=1)
    conv = lax.conv_general_dilated(u_ext, w_conv[:, None, :].astype(u.dtype), window_strides=(1,),
                                    padding='VALID', dimension_numbers=('NWC', 'WIO', 'NWC'),
                                    feature_group_count=D_MODEL)
    return (b_gate * conv) @ w_out, u_ext[:, -(CONV_WIDTH - 1):]


def moe_block_size(n_assign):
    b = 8
    while b < MOE_MAX_BLOCK and b * 2 * N_EXPERTS <= n_assign:
        b *= 2
    return b


def routed_experts(x, idx, gate, w_gate, w_up, w_down):
    T = x.shape[0]
    n_assign = T * TOP_K
    blk = moe_block_size(n_assign)
    n_blocks = -(-n_assign // blk) + N_EXPERTS
    e_flat = idx.reshape(n_assign)
    tok_flat = jnp.repeat(jnp.arange(T, dtype=jnp.int32), TOP_K)
    g_flat = gate.reshape(n_assign)
    order = jnp.argsort(e_flat)
    e_sorted = e_flat[order]
    counts = jnp.bincount(e_flat, length=N_EXPERTS)
    padded = (counts + blk - 1) // blk * blk
    pad_end = jnp.cumsum(padded)
    pad_start = pad_end - padded
    start = jnp.cumsum(counts) - counts
    dest = pad_start[e_sorted] + jnp.arange(n_assign) - start[e_sorted]
    row_tok = jnp.zeros((n_blocks * blk,), jnp.int32).at[dest].set(tok_flat[order])
    row_gate = jnp.zeros((n_blocks * blk,), x.dtype).at[dest].set(g_flat[order])
    block_expert = jnp.minimum(jnp.searchsorted(pad_end, jnp.arange(n_blocks) * blk, side='right'), N_EXPERTS - 1)

    def one_block(args):
        toks, e = args
        xb = x[toks]
        hid = jax.nn.silu(xb @ w_gate[e]) * (xb @ w_up[e])
        return hid @ w_down[e]

    out = lax.map(one_block, (row_tok.reshape(n_blocks, blk), block_expert))
    out = out.reshape(n_blocks * blk, D_MODEL) * row_gate[:, None]
    return jax.ops.segment_sum(out, row_tok, num_segments=T)


def moe_ffn(h, w_router, b_router, w_gate, w_up, w_down, ws_gate, ws_up, ws_down):
    shape = h.shape
    x = h.reshape(-1, D_MODEL)
    T = x.shape[0]
    scores = jax.nn.sigmoid(jnp.dot(x, w_router, preferred_element_type=jnp.float32))
    biased = scores + b_router.astype(jnp.float32)
    grp = biased.reshape(T, N_GROUPS, N_EXPERTS // N_GROUPS)
    grp_score = lax.top_k(grp, 2)[0].sum(-1)
    _, top_grp = lax.top_k(grp_score, TOPK_GROUPS)
    grp_mask = (top_grp[:, :, None] == jnp.arange(N_GROUPS)[None, None, :]).any(axis=1)
    expert_mask = jnp.repeat(grp_mask, N_EXPERTS // N_GROUPS, axis=1)
    _, idx = lax.top_k(jnp.where(expert_mask, biased, -jnp.inf), TOP_K)
    w = jnp.take_along_axis(scores, idx, axis=1)
    w = w / w.sum(-1, keepdims=True) * ROUTED_SCALE
    routed = routed_experts(x, idx, w.astype(x.dtype), w_gate, w_up, w_down)
    shared = (jax.nn.silu(x @ ws_gate) * (x @ ws_up)) @ ws_down
    return (routed + shared).reshape(shape)


def setup_inputs(seed: int = 0) -> dict:
    key = jax.random.key(seed)
    ks = jax.random.split(key, 28)
    f32 = jnp.float32
    D = D_MODEL
    HD = N_HEADS * HEAD_DIM
    n_pages = PAST_LEN // PAGE_SIZE
    used = DEC_BATCH * n_pages
    n_pool = used + max(1, used // 4)

    def nrm(k, shape, scale):
        return jax.random.normal(k, shape, f32) * scale

    page_table = jax.random.permutation(ks[0], n_pool)[:used].reshape(DEC_BATCH, n_pages).astype(jnp.int32)
    return {
        "x_prompt": nrm(ks[1], (BATCH, SEQ, D), 1.0),
        "x_sample": nrm(ks[2], (DEC_BATCH, DEC_SEQ, D), 1.0),
        "cache_k": nrm(ks[3], (n_pool, N_MOBA_LAYERS, PAGE_SIZE, N_HEADS, HEAD_DIM), 1.0),
        "cache_v": nrm(ks[4], (n_pool, N_MOBA_LAYERS, PAGE_SIZE, N_HEADS, HEAD_DIM), 1.0),
        "state_conv": nrm(ks[5], (DEC_BATCH, N_CONV_LAYERS, CONV_WIDTH - 1, D), 1.0),
        "page_table": page_table,
        "c_prompt": nrm(ks[6], (BATCH, D), 1.0),
        "c_sample": nrm(ks[7], (DEC_BATCH, D), 1.0),
        "ada_w": nrm(ks[8], (DEPTH, 2, D, 3 * D), ADA_SCALE * D ** -0.5),
        "ada_b": nrm(ks[9], (DEPTH, 2, 3 * D), 0.02),
        "ln_g": 1.0 + nrm(ks[10], (DEPTH, 2, D), 0.02),
        "ln_b": nrm(ks[11], (DEPTH, 2, D), 0.02),
        "attn_w_qkv": nrm(ks[12], (N_MOBA_LAYERS, D, 3 * HD), D ** -0.5),
        "attn_w_o": nrm(ks[13], (N_MOBA_LAYERS, HD, D), DEEPNORM_BETA * HD ** -0.5),
        "conv_w_in": nrm(ks[14], (N_CONV_LAYERS, D, 3 * D), D ** -0.5),
        "conv_w": nrm(ks[15], (N_CONV_LAYERS, CONV_WIDTH, D), CONV_WIDTH ** -0.5),
        "conv_w_out": nrm(ks[16], (N_CONV_LAYERS, D, D), DEEPNORM_BETA * D ** -0.5),
        "router_w": nrm(ks[17], (DEPTH, D, N_EXPERTS), D ** -0.5),
        "router_b": nrm(ks[18], (DEPTH, N_EXPERTS), 0.01),
        "exp_w_gate": nrm(ks[19], (DEPTH, N_EXPERTS, D, D_EXPERT), D ** -0.5),
        "exp_w_up": nrm(ks[20], (DEPTH, N_EXPERTS, D, D_EXPERT), D ** -0.5),
        "exp_w_down": nrm(ks[21], (DEPTH, N_EXPERTS, D_EXPERT, D), DEEPNORM_BETA * D_EXPERT ** -0.5),
        "shared_w_gate": nrm(ks[22], (DEPTH, D, D_SHARED), D ** -0.5),
        "shared_w_up": nrm(ks[23], (DEPTH, D, D_SHARED), D ** -0.5),
        "shared_w_down": nrm(ks[24], (DEPTH, D_SHARED, D), DEEPNORM_BETA * D_SHARED ** -0.5),
    }


def reference(x_prompt, x_sample, cache_k, cache_v, state_conv, page_table, c_prompt, c_sample,
              ada_w, ada_b, ln_g, ln_b, attn_w_qkv, attn_w_o, conv_w_in, conv_w, conv_w_out,
              router_w, router_b, exp_w_gate, exp_w_up, exp_w_down,
              shared_w_gate, shared_w_up, shared_w_down):
    x_p, x_s = x_prompt, x_sample
    k_p_rows, v_p_rows, k_s_rows, v_s_rows, conv_p_rows, conv_s_rows = [], [], [], [], [], []
    for i in range(DEPTH):
        li = i // N_MIXERS
        sh_p, sc_p, g_p = ada_modulation(c_prompt, ada_w[i, 0], ada_b[i, 0])
        sh_s, sc_s, g_s = ada_modulation(c_sample, ada_w[i, 0], ada_b[i, 0])
        h_p = x_p * (1 + sc_p) + sh_p
        h_s = x_s * (1 + sc_s) + sh_s
        if i % N_MIXERS == 0:
            y_p, k_p, v_p = moba_prompt(h_p, attn_w_qkv[li], attn_w_o[li])
            y_s, k_s, v_s = moba_sample(h_s, cache_k, cache_v, page_table, li, attn_w_qkv[li], attn_w_o[li])
            k_p_rows.append(k_p)
            v_p_rows.append(v_p)
            k_s_rows.append(k_s)
            v_s_rows.append(v_s)
        else:
            hist_p = jnp.zeros((x_p.shape[0], CONV_WIDTH - 1, D_MODEL), h_p.dtype)
            y_p, st_p = short_conv(h_p, hist_p, conv_w_in[li], conv_w[li], conv_w_out[li])
            y_s, st_s = short_conv(h_s, state_conv[:, li], conv_w_in[li], conv_w[li], conv_w_out[li])
            conv_p_rows.append(st_p)
            conv_s_rows.append(st_s)
        x_p = post_norm_residual(x_p, y_p, g_p, ln_g[i, 0], ln_b[i, 0])
        x_s = post_norm_residual(x_s, y_s, g_s, ln_g[i, 0], ln_b[i, 0])
        sh_p, sc_p, g_p = ada_modulation(c_prompt, ada_w[i, 1], ada_b[i, 1])
        sh_s, sc_s, g_s = ada_modulation(c_sample, ada_w[i, 1], ada_b[i, 1])
        f_p = moe_ffn(x_p * (1 + sc_p) + sh_p, router_w[i], router_b[i], exp_w_gate[i], exp_w_up[i], exp_w_down[i],
                      shared_w_gate[i], shared_w_up[i], shared_w_down[i])
        f_s = moe_ffn(x_s * (1 + sc_s) + sh_s, router_w[i], router_b[i], exp_w_gate[i], exp_w_up[i], exp_w_down[i],
                      shared_w_gate[i], shared_w_up[i], shared_w_down[i])
        x_p = post_norm_residual(x_p, f_p, g_p, ln_g[i, 1], ln_b[i, 1])
        x_s = post_norm_residual(x_s, f_s, g_s, ln_g[i, 1], ln_b[i, 1])
    return (x_p, x_s, jnp.stack(k_p_rows, axis=1), jnp.stack(v_p_rows, axis=1), jnp.stack(conv_p_rows, axis=1),
            jnp.stack(k_s_rows, axis=1), jnp.stack(v_s_rows, axis=1), jnp.stack(conv_s_rows, axis=1))
```

```python
import functools

import jax
import jax.numpy as jnp
from jax import lax
from jax.experimental import pallas as pl
from jax.experimental.pallas import tpu as pltpu

D_MODEL = 2048
DEPTH = 4
PAGE_SIZE = 128
N_HEADS = 16
HEAD_DIM = D_MODEL // N_HEADS
ROPE_THETA = 10000.0
MOBA_BLOCK = 256
MOBA_TOPK = 3
CONV_WIDTH = 3
N_EXPERTS = 64
TOP_K = 8
N_GROUPS = 8
TOPK_GROUPS = 4
D_EXPERT = 512
ROUTED_SCALE = 2.5
DEEPNORM_ALPHA = (2 * DEPTH) ** 0.25
LN_EPS = 1e-5

F32 = jnp.float32
BF16 = jnp.bfloat16
NEG_INF = float("-inf")

VMEM_LIMIT_BYTES = 56 * 1024 * 1024
MOE_ROW_BLOCK = 256
N_COND_ROWS = 16


def _params(n_axes):
    return pltpu.CompilerParams(dimension_semantics=("arbitrary",) * n_axes,
                                vmem_limit_bytes=VMEM_LIMIT_BYTES)


def _silu(x):
    return x / (1.0 + jnp.exp(-x))


def _split_bf16(x):
    hi = x.astype(BF16)
    lo = (x - hi.astype(F32)).astype(BF16)
    return hi, lo


def _dot(a, b):
    return jnp.dot(a, b, preferred_element_type=F32)


def _dot3(a, b):
    a_hi, a_lo = _split_bf16(a)
    b_hi, b_lo = _split_bf16(b)
    return _dot(a_hi, b_hi) + _dot(a_lo, b_hi) + _dot(a_hi, b_lo)


def _ada_kernel(c_ref, w_ref, b_ref, o_ref):
    o_ref[0] = _dot3(_silu(c_ref[...]), w_ref[0]) + b_ref[0]


def ada_modulation_all(c_all, ada_w, ada_b):
    n_sub = ada_w.shape[0]
    tn = 768
    return pl.pallas_call(
        _ada_kernel,
        grid=(n_sub, 3 * D_MODEL // tn),
        in_specs=[
            pl.BlockSpec((N_COND_ROWS, D_MODEL), lambda s, j: (0, 0)),
            pl.BlockSpec((1, D_MODEL, tn), lambda s, j: (s, 0, j)),
            pl.BlockSpec((1, 1, tn), lambda s, j: (s, 0, j)),
        ],
        out_specs=pl.BlockSpec((1, N_COND_ROWS, tn), lambda s, j: (s, 0, j)),
        out_shape=jax.ShapeDtypeStruct((n_sub, N_COND_ROWS, 3 * D_MODEL), F32),
        compiler_params=_params(2),
        name="ada_modulation",
    )(c_all, ada_w, ada_b)


def _proj_kernel(x_ref, sc_ref, sh_ref, w_ref, cos_ref, sin_ref, o_ref, hb_ref, *, rope, tn):
    c = pl.program_id(2)
    j = pl.program_id(3)

    @pl.when((c == 0) & (j == 0))
    def _():
        hb_ref[...] = (x_ref[0] * (1.0 + sc_ref[0]) + sh_ref[0]).astype(BF16)

    acc = _dot(hb_ref[...], w_ref[...].astype(BF16))
    if rope:
        @pl.when(c < 2)
        def _():
            cos = cos_ref[...]
            sin = sin_ref[...]
            for h in range(tn // HEAD_DIM):
                seg = acc[:, h * HEAD_DIM:(h + 1) * HEAD_DIM]
                o_ref[0, 0, :, h * HEAD_DIM:(h + 1) * HEAD_DIM] = (
                    seg * cos + pltpu.roll(seg, HEAD_DIM // 2, axis=1) * sin)

        @pl.when(c == 2)
        def _():
            o_ref[0, 0] = acc
    else:
        o_ref[0, 0] = acc


def modulated_projection(x, sc, sh, w, cos, sin, *, rope, tm):
    nb, seq, _ = x.shape
    r = sc.shape[1]
    tn = 512
    nj = D_MODEL // tn
    return pl.pallas_call(
        functools.partial(_proj_kernel, rope=rope, tn=tn),
        grid=(nb, seq // tm, 3, nj),
        in_specs=[
            pl.BlockSpec((1, tm, D_MODEL), lambda b, m, c, j: (b, m, 0)),
            pl.BlockSpec((1, r, D_MODEL), lambda b, m, c, j: (b, 0, 0)),
            pl.BlockSpec((1, r, D_MODEL), lambda b, m, c, j: (b, 0, 0)),
            pl.BlockSpec((D_MODEL, tn), lambda b, m, c, j: (0, c * nj + j)),
            pl.BlockSpec((tm, HEAD_DIM), lambda b, m, c, j: (m, 0)),
            pl.BlockSpec((tm, HEAD_DIM), lambda b, m, c, j: (m, 0)),
        ],
        out_specs=pl.BlockSpec((1, 1, tm, tn), lambda b, m, c, j: (c, b, m, j)),
        out_shape=jax.ShapeDtypeStruct((3, nb, seq, D_MODEL), F32),
        scratch_shapes=[pltpu.VMEM((tm, D_MODEL), BF16)],
        compiler_params=_params(4),
        name="modulated_projection",
    )(x, sc, sh, w, cos, sin)


def _moba_prompt_kernel(q_ref, k_ref, v_ref, o_ref, kb_ref, vt_ref, kmean_ref, bias_ref, *, n_blocks):
    qi = pl.program_id(2)

    @pl.when(qi == 0)
    def _():
        for n in range(n_blocks):
            k_blk = k_ref[0, 0, n * MOBA_BLOCK:(n + 1) * MOBA_BLOCK, :]
            kb_ref[n] = k_blk.astype(BF16)
            kmean_ref[n:n + 1, :] = jnp.sum(k_blk, axis=0, keepdims=True) * (1.0 / MOBA_BLOCK)
            vt_ref[n] = v_ref[0, 0, n * MOBA_BLOCK:(n + 1) * MOBA_BLOCK, :].T.astype(BF16)

    q = q_ref[0, 0]
    q_t = q.T
    sc = _dot3(kmean_ref[...], q_t)
    blk = lax.broadcasted_iota(jnp.int32, sc.shape, 0).astype(F32)
    sc = jnp.where(blk < qi.astype(F32), sc, NEG_INF)
    bias = jnp.full(sc.shape, NEG_INF, F32)
    for _ in range(MOBA_TOPK):
        best = jnp.max(sc, axis=0, keepdims=True)
        first = jnp.min(jnp.where(sc == best, blk, float(n_blocks)), axis=0, keepdims=True)
        pick = (blk == first) & (best > NEG_INF)
        bias = jnp.where(pick, 0.0, bias)
        sc = jnp.where(pick, NEG_INF, sc)
    bias_ref[...] = bias

    qs_t = (q * (HEAD_DIM ** -0.5)).T.astype(BF16)
    s = _dot(kb_ref[qi], qs_t)
    key_i = lax.broadcasted_iota(jnp.int32, s.shape, 0)
    qry_i = lax.broadcasted_iota(jnp.int32, s.shape, 1)
    s = jnp.where(key_i <= qry_i, s, NEG_INF)
    m0 = jnp.max(s, axis=0, keepdims=True)
    p = jnp.exp(s - m0)
    l0 = jnp.sum(p, axis=0, keepdims=True)
    acc0 = _dot(vt_ref[qi], p.astype(BF16))

    def body(n, carry):
        m, l, acc = carry
        s = _dot(kb_ref[n], qs_t) + bias_ref[pl.ds(n, 1), :]
        m_new = jnp.maximum(m, jnp.max(s, axis=0, keepdims=True))
        alpha = jnp.exp(m - m_new)
        p = jnp.exp(s - m_new)
        l = alpha * l + jnp.sum(p, axis=0, keepdims=True)
        acc = alpha * acc + _dot(vt_ref[n], p.astype(BF16))
        return m_new, l, acc

    _, l, acc = lax.fori_loop(0, qi, body, (m0, l0, acc0))
    o_ref[0] = (acc / l).T


def moba_prompt_attention(qkv):
    _, nb, seq, _ = qkv.shape
    n_blocks = seq // MOBA_BLOCK
    return pl.pallas_call(
        functools.partial(_moba_prompt_kernel, n_blocks=n_blocks),
        grid=(nb, N_HEADS, n_blocks),
        in_specs=[
            pl.BlockSpec((1, 1, MOBA_BLOCK, HEAD_DIM), lambda b, h, i: (0, b, i, h)),
            pl.BlockSpec((1, 1, seq, HEAD_DIM), lambda b, h, i: (1, b, 0, h)),
            pl.BlockSpec((1, 1, seq, HEAD_DIM), lambda b, h, i: (2, b, 0, h)),
        ],
        out_specs=pl.BlockSpec((1, MOBA_BLOCK, HEAD_DIM), lambda b, h, i: (b, i, h)),
        out_shape=jax.ShapeDtypeStruct((nb, seq, D_MODEL), F32),
        scratch_shapes=[
            pltpu.VMEM((n_blocks, MOBA_BLOCK, HEAD_DIM), BF16),
            pltpu.VMEM((n_blocks, HEAD_DIM, MOBA_BLOCK), BF16),
            pltpu.VMEM((n_blocks, HEAD_DIM), F32),
            pltpu.VMEM((n_blocks, MOBA_BLOCK), F32),
        ],
        compiler_params=_params(3),
        name="moba_prompt_attention",
    )(qkv, qkv, qkv)


def _kmean_kernel(pt_ref, p0_ref, p1_ref, o_ref):
    del pt_ref
    tot = jnp.sum(p0_ref[0, 0], axis=0, keepdims=True) + jnp.sum(p1_ref[0, 0], axis=0, keepdims=True)
    o_ref[0, 0] = tot * (1.0 / MOBA_BLOCK)


def paged_block_means(cache_k, page_table, li):
    nb, n_pages = page_table.shape
    ppb = MOBA_BLOCK // PAGE_SIZE
    assert ppb == 2
    n_blocks = n_pages // ppb
    page_spec = lambda p: pl.BlockSpec(
        (1, 1, PAGE_SIZE, D_MODEL), lambda b, n, pt: (pt[b, n * ppb + p], li, 0, 0))
    return pl.pallas_call(
        _kmean_kernel,
        grid_spec=pltpu.PrefetchScalarGridSpec(
            num_scalar_prefetch=1,
            grid=(nb, n_blocks),
            in_specs=[page_spec(0), page_spec(1)],
            out_specs=pl.BlockSpec((1, 1, 1, D_MODEL), lambda b, n, pt: (b, n, 0, 0)),
        ),
        out_shape=jax.ShapeDtypeStruct((nb, n_blocks, 1, D_MODEL), F32),
        compiler_params=_params(2),
        name="paged_block_means",
    )(page_table, cache_k, cache_k)


def _sample_select_kernel(q_ref, km_ref, o_ref, *, n_blocks):
    lane = lax.broadcasted_iota(jnp.int32, (8, 128), 1)
    for h in range(N_HEADS):
        q = q_ref[0, :, h * HEAD_DIM:(h + 1) * HEAD_DIM]
        km = km_ref[0, :, h * HEAD_DIM:(h + 1) * HEAD_DIM]
        q_hi, q_lo = _split_bf16(q)
        k_hi, k_lo = _split_bf16(km)
        nt = (((1,), (1,)), ((), ()))
        sc = (lax.dot_general(q_hi, k_hi, nt, preferred_element_type=F32)
              + lax.dot_general(q_lo, k_hi, nt, preferred_element_type=F32)
              + lax.dot_general(q_hi, k_lo, nt, preferred_element_type=F32))
        blk = lax.broadcasted_iota(jnp.int32, sc.shape, 1).astype(F32)
        out = jnp.zeros((8, 128), F32)
        for t in range(MOBA_TOPK):
            best = jnp.max(sc, axis=1, keepdims=True)
            first = jnp.min(jnp.where(sc == best, blk, float(n_blocks)), axis=1, keepdims=True)
            out = jnp.where(lane == t, first, out)
            sc = jnp.where(blk == first, NEG_INF, sc)
        o_ref[0, h] = out.astype(jnp.int32)


def sample_block_select(q_pad, kmean):
    nb, n_blocks, _ = kmean.shape
    return pl.pallas_call(
        functools.partial(_sample_select_kernel, n_blocks=n_blocks),
        grid=(nb,),
        in_specs=[
            pl.BlockSpec((1, 8, D_MODEL), lambda b: (b, 0, 0)),
            pl.BlockSpec((1, n_blocks, D_MODEL), lambda b: (b, 0, 0)),
        ],
        out_specs=pl.BlockSpec((1, N_HEADS, 8, 128), lambda b: (b, 0, 0, 0)),
        out_shape=jax.ShapeDtypeStruct((nb, N_HEADS, 8, 128), jnp.int32),
        compiler_params=_params(1),
        name="sample_block_select",
    )(q_pad, kmean)


def _sample_attn_kernel(phys_ref, q_ref, ko_ref, vo_ref, *refs, n_pages, dec_seq):
    del phys_ref
    k_refs = refs[:n_pages]
    v_refs = refs[n_pages:2 * n_pages]
    o_ref = refs[2 * n_pages]
    l_idx = pl.program_id(2)
    qs = q_ref[0, pl.ds(l_idx, 1), :] * (HEAD_DIM ** -0.5)
    q8 = jnp.broadcast_to(qs, (8, HEAD_DIM)).astype(BF16)
    nt = (((1,), (1,)), ((), ()))
    s_sel = [lax.dot_general(q8, k_refs[p][0, 0].astype(BF16), nt, preferred_element_type=F32)[0:1]
             for p in range(n_pages)]
    k_own = ko_ref[0]
    s_own = jnp.sum(k_own * qs, axis=1, keepdims=True)
    row = lax.broadcasted_iota(jnp.int32, s_own.shape, 0)
    s_own = jnp.where(row <= l_idx, s_own, NEG_INF)
    m = jnp.max(s_own, axis=0, keepdims=True)
    for s in s_sel:
        m = jnp.maximum(m, jnp.max(s, axis=1, keepdims=True))
    p_own = jnp.exp(s_own - m)
    denom = jnp.sum(p_own, axis=0, keepdims=True)
    acc = jnp.sum(p_own * vo_ref[0], axis=0, keepdims=True)
    for p in range(n_pages):
        pr = jnp.exp(s_sel[p] - m)
        denom = denom + jnp.sum(pr, axis=1, keepdims=True)
        p8 = jnp.broadcast_to(pr, (8, PAGE_SIZE)).astype(BF16)
        acc = acc + _dot(p8, v_refs[p][0, 0].astype(BF16))[0:1]
    o_ref[0, pl.ds(l_idx, 1), :] = acc / denom


def moba_sample_attention(q, k_new, v_new, cache_k, cache_v, phys, li):
    nb, dec_seq, _ = q.shape
    n_pages = phys.shape[-1]
    page_spec = lambda p: pl.BlockSpec(
        (1, 1, PAGE_SIZE, HEAD_DIM),
        lambda b, h, l, ph: (ph[((b * dec_seq + l) * N_HEADS + h) * n_pages + p], li, 0, h))
    own_spec = pl.BlockSpec((1, dec_seq, HEAD_DIM), lambda b, h, l, ph: (b, 0, h))
    return pl.pallas_call(
        functools.partial(_sample_attn_kernel, n_pages=n_pages, dec_seq=dec_seq),
        grid_spec=pltpu.PrefetchScalarGridSpec(
            num_scalar_prefetch=1,
            grid=(nb, N_HEADS, dec_seq),
            in_specs=[own_spec, own_spec, own_spec]
            + [page_spec(p) for p in range(n_pages)] * 2,
            out_specs=own_spec,
        ),
        out_shape=jax.ShapeDtypeStruct((nb, dec_seq, D_MODEL), F32),
        compiler_params=_params(3),
        name="moba_sample_attention",
    )(phys.reshape(-1), q, k_new, v_new, *([cache_k] * n_pages), *([cache_v] * n_pages))


CONV_PAD_ROWS = 8


def _conv_gate_kernel(*refs, tl, has_halo):
    if has_halo:
        b_ref, c_ref, v_ref, ch_ref, vh_ref, hist_ref, w_ref, z_ref, st_ref, ext_ref = refs
    else:
        b_ref, c_ref, v_ref, hist_ref, w_ref, z_ref, st_ref, ext_ref = refs
    l_idx = pl.program_id(1)
    u = c_ref[0, 0] * v_ref[0, 0]
    ext_ref[CONV_PAD_ROWS:CONV_PAD_ROWS + tl, :] = u
    n_hist = CONV_WIDTH - 1

    @pl.when(l_idx == 0)
    def _():
        ext_ref[CONV_PAD_ROWS - n_hist:CONV_PAD_ROWS, :] = hist_ref[0]

    if has_halo:
        @pl.when(l_idx > 0)
        def _():
            ext_ref[0:CONV_PAD_ROWS, :] = ch_ref[0, 0] * vh_ref[0, 0]

    w = w_ref[...]
    conv = u * w[2:3, :]
    conv = conv + ext_ref[CONV_PAD_ROWS - 1:CONV_PAD_ROWS - 1 + tl, :] * w[1:2, :]
    conv = conv + ext_ref[CONV_PAD_ROWS - 2:CONV_PAD_ROWS - 2 + tl, :] * w[0:1, :]
    z_ref[0] = b_ref[0, 0] * conv

    @pl.when(l_idx == pl.num_programs(1) - 1)
    def _():
        st_ref[0] = ext_ref[CONV_PAD_ROWS + tl - n_hist:CONV_PAD_ROWS + tl, :]


def conv_gate(bcv, hist, w_conv, *, tl):
    _, nb, seq, _ = bcv.shape
    n_hist = CONV_WIDTH - 1
    nl = seq // tl
    has_halo = nl > 1
    part = lambda c: pl.BlockSpec((1, 1, tl, D_MODEL), lambda b, l: (c, b, l, 0))
    halo = lambda c: pl.BlockSpec(
        (1, 1, CONV_PAD_ROWS, D_MODEL),
        lambda b, l: (c, b, jnp.maximum(l * (tl // CONV_PAD_ROWS) - 1, 0), 0))
    in_specs = [part(0), part(1), part(2)]
    args = [bcv, bcv, bcv]
    if has_halo:
        in_specs += [halo(1), halo(2)]
        args += [bcv, bcv]
    in_specs += [pl.BlockSpec((1, n_hist, D_MODEL), lambda b, l: (b, 0, 0)),
                 pl.BlockSpec((CONV_WIDTH, D_MODEL), lambda b, l: (0, 0))]
    args += [hist, w_conv]
    return pl.pallas_call(
        functools.partial(_conv_gate_kernel, tl=tl, has_halo=has_halo),
        grid=(nb, nl),
        in_specs=in_specs,
        out_specs=[pl.BlockSpec((1, tl, D_MODEL), lambda b, l: (b, l, 0)),
                   pl.BlockSpec((1, n_hist, D_MODEL), lambda b, l: (b, 0, 0))],
        out_shape=[jax.ShapeDtypeStruct((nb, seq, D_MODEL), F32),
                   jax.ShapeDtypeStruct((nb, n_hist, D_MODEL), F32)],
        scratch_shapes=[pltpu.VMEM((CONV_PAD_ROWS + tl, D_MODEL), F32)],
        compiler_params=_params(2),
        name="conv_gate",
    )(*args)


def _layer_norm(z, g, b):
    mu = jnp.mean(z, axis=-1, keepdims=True)
    d = z - mu
    var = jnp.mean(d * d, axis=-1, keepdims=True)
    return d * lax.rsqrt(var + LN_EPS) * g + b


def _out_norm_kernel(a_ref, w_ref, x_ref, gate_ref, g_ref, b_ref, sc_ref, sh_ref, wr_ref,
                     xo_ref, h_ref, score_ref, acc_ref):
    k = pl.program_id(2)

    @pl.when(k == 0)
    def _():
        acc_ref[...] = jnp.zeros_like(acc_ref)

    acc_ref[...] += _dot(a_ref[0].astype(BF16), w_ref[...].astype(BF16))

    @pl.when(k == pl.num_programs(2) - 1)
    def _():
        xn = _layer_norm(DEEPNORM_ALPHA * x_ref[0] + gate_ref[0] * acc_ref[...], g_ref[...], b_ref[...])
        xo_ref[0] = xn
        h = xn * (1.0 + sc_ref[0]) + sh_ref[0]
        h_ref[0] = h
        logits = _dot3(h, wr_ref[...])
        score_ref[0] = 1.0 / (1.0 + jnp.exp(-logits))


def out_projection_norm(a, w, x, gate, ln_g, ln_b, sc_next, sh_next, w_router, *, tm):
    nb, seq, kdim = a.shape
    r = gate.shape[1]
    tk = 512
    row = pl.BlockSpec((1, tm, D_MODEL), lambda b, m, k: (b, m, 0))
    mod = pl.BlockSpec((1, r, D_MODEL), lambda b, m, k: (b, 0, 0))
    vec = pl.BlockSpec((1, D_MODEL), lambda b, m, k: (0, 0))
    return pl.pallas_call(
        _out_norm_kernel,
        grid=(nb, seq // tm, kdim // tk),
        in_specs=[
            pl.BlockSpec((1, tm, tk), lambda b, m, k: (b, m, k)),
            pl.BlockSpec((tk, D_MODEL), lambda b, m, k: (k, 0)),
            row, mod, vec, vec, mod, mod,
            pl.BlockSpec((D_MODEL, N_EXPERTS), lambda b, m, k: (0, 0)),
        ],
        out_specs=[row, row, pl.BlockSpec((1, tm, N_EXPERTS), lambda b, m, k: (b, m, 0))],
        out_shape=[jax.ShapeDtypeStruct((nb, seq, D_MODEL), F32),
                   jax.ShapeDtypeStruct((nb, seq, D_MODEL), F32),
                   jax.ShapeDtypeStruct((nb, seq, N_EXPERTS), F32)],
        scratch_shapes=[pltpu.VMEM((tm, D_MODEL), F32)],
        compiler_params=_params(3),
        name="out_projection_norm",
    )(a, w, x, gate, ln_g, ln_b, sc_next, sh_next, w_router)


def _swiglu(x_bf16, wg, wu, wd):
    g = _dot(x_bf16, wg)
    u = _dot(x_bf16, wu)
    return _dot((_silu(g) * u).astype(BF16), wd)


def _experts_kernel(be_ref, nb_ref, xs_ref, rg_ref, wg_ref, wu_ref, wd_ref, ys_ref, wgb, wub, wdb):
    i = pl.program_id(0)
    active = i < nb_ref[0]
    new_expert = (i == 0) | (be_ref[i] != be_ref[jnp.maximum(i - 1, 0)])

    @pl.when(active & new_expert)
    def _():
        wgb[...] = wg_ref[0].astype(BF16)
        wub[...] = wu_ref[0].astype(BF16)
        wdb[...] = wd_ref[0].astype(BF16)

    @pl.when(active)
    def _():
        y = _swiglu(xs_ref[...].astype(BF16), wgb[...], wub[...], wdb[...])
        ys_ref[...] = y * rg_ref[...]

    @pl.when(jnp.logical_not(active))
    def _():
        ys_ref[...] = jnp.zeros_like(ys_ref)


def routed_expert_rows(block_expert, n_used_blocks, xs, row_gate, w_gate, w_up, w_down):
    n_rows = xs.shape[0]
    bm = MOE_ROW_BLOCK
    wspec = lambda shape: pl.BlockSpec((1,) + shape, lambda i, be, nb: (be[i], 0, 0))
    return pl.pallas_call(
        _experts_kernel,
        grid_spec=pltpu.PrefetchScalarGridSpec(
            num_scalar_prefetch=2,
            grid=(n_rows // bm,),
            in_specs=[
                pl.BlockSpec((bm, D_MODEL), lambda i, be, nb: (i, 0)),
                pl.BlockSpec((bm, 1), lambda i, be, nb: (i, 0)),
                wspec((D_MODEL, D_EXPERT)), wspec((D_MODEL, D_EXPERT)), wspec((D_EXPERT, D_MODEL)),
            ],
            out_specs=pl.BlockSpec((bm, D_MODEL), lambda i, be, nb: (i, 0)),
            scratch_shapes=[pltpu.VMEM((D_MODEL, D_EXPERT), BF16),
                            pltpu.VMEM((D_MODEL, D_EXPERT), BF16),
                            pltpu.VMEM((D_EXPERT, D_MODEL), BF16)],
        ),
        out_shape=jax.ShapeDtypeStruct((n_rows, D_MODEL), F32),
        compiler_params=_params(1),
        name="routed_experts",
    )(block_expert, n_used_blocks, xs, row_gate, w_gate, w_up, w_down)


def _shared_norm_kernel(h_ref, x_ref, r_ref, gate_ref, g_ref, b_ref, wg_ref, wu_ref, wd_ref,
                        xo_ref, wgb, wub, wdb):
    @pl.when((pl.program_id(0) == 0) & (pl.program_id(1) == 0))
    def _():
        wgb[...] = wg_ref[...].astype(BF16)
        wub[...] = wu_ref[...].astype(BF16)
        wdb[...] = wd_ref[...].astype(BF16)

    f = r_ref[0] + _swiglu(h_ref[0].astype(BF16), wgb[...], wub[...], wdb[...])
    xo_ref[0] = _layer_norm(DEEPNORM_ALPHA * x_ref[0] + gate_ref[0] * f, g_ref[...], b_ref[...])


def shared_expert_norm(h, x, routed, gate, ln_g, ln_b, ws_gate, ws_up, ws_down, *, tm):
    nb, seq, _ = h.shape
    r = gate.shape[1]
    d_shared = ws_gate.shape[1]
    row = pl.BlockSpec((1, tm, D_MODEL), lambda b, m: (b, m, 0))
    vec = pl.BlockSpec((1, D_MODEL), lambda b, m: (0, 0))
    whole = lambda shape: pl.BlockSpec(shape, lambda b, m: (0, 0), pipeline_mode=pl.Buffered(1))
    return pl.pallas_call(
        _shared_norm_kernel,
        grid=(nb, seq // tm),
        in_specs=[row, row, row, pl.BlockSpec((1, r, D_MODEL), lambda b, m: (b, 0, 0)), vec, vec,
                  whole((D_MODEL, d_shared)), whole((D_MODEL, d_shared)), whole((d_shared, D_MODEL))],
        out_specs=row,
        out_shape=jax.ShapeDtypeStruct((nb, seq, D_MODEL), F32),
        scratch_shapes=[pltpu.VMEM((D_MODEL, d_shared), BF16),
                        pltpu.VMEM((D_MODEL, d_shared), BF16),
                        pltpu.VMEM((d_shared, D_MODEL), BF16)],
        compiler_params=_params(2),
        name="shared_expert_norm",
    )(h, x, routed, gate, ln_g, ln_b, ws_gate, ws_up, ws_down)


def _route(scores, b_router):
    n_tok = scores.shape[0]
    biased = scores + b_router.astype(F32)
    grp = biased.reshape(n_tok, N_GROUPS, N_EXPERTS // N_GROUPS)
    grp_score = lax.top_k(grp, 2)[0].sum(-1)
    _, top_grp = lax.top_k(grp_score, TOPK_GROUPS)
    grp_mask = (top_grp[:, :, None] == jnp.arange(N_GROUPS)[None, None, :]).any(axis=1)
    expert_mask = jnp.repeat(grp_mask, N_EXPERTS // N_GROUPS, axis=1)
    _, idx = lax.top_k(jnp.where(expert_mask, biased, -jnp.inf), TOP_K)
    w = jnp.take_along_axis(scores, idx, axis=1)
    w = w / w.sum(-1, keepdims=True) * ROUTED_SCALE
    return idx, w


def _dispatch(idx, gate):
    n_tok = idx.shape[0]
    n_assign = n_tok * TOP_K
    bm = MOE_ROW_BLOCK
    n_blocks = -(-n_assign // bm) + N_EXPERTS
    e_flat = idx.reshape(n_assign)
    tok_flat = jnp.repeat(jnp.arange(n_tok, dtype=jnp.int32), TOP_K)
    order = jnp.argsort(e_flat)
    e_sorted = e_flat[order]
    counts = jnp.bincount(e_flat, length=N_EXPERTS)
    padded = (counts + bm - 1) // bm * bm
    pad_end = jnp.cumsum(padded)
    pad_start = pad_end - padded
    start = jnp.cumsum(counts) - counts
    dest = (pad_start[e_sorted] + jnp.arange(n_assign) - start[e_sorted]).astype(jnp.int32)
    row_tok = jnp.zeros((n_blocks * bm,), jnp.int32).at[dest].set(tok_flat[order])
    row_gate = jnp.zeros((n_blocks * bm,), F32).at[dest].set(gate.reshape(n_assign)[order])
    block_expert = jnp.minimum(
        jnp.searchsorted(pad_end, jnp.arange(n_blocks) * bm, side='right'), N_EXPERTS - 1).astype(jnp.int32)
    n_used = (pad_end[-1] // bm).astype(jnp.int32).reshape(1)
    pos = jnp.zeros((n_assign,), jnp.int32).at[order].set(dest).reshape(n_tok, TOP_K)
    return row_tok, row_gate, block_expert, n_used, pos


def _rope_tables(pos):
    half = HEAD_DIM // 2
    inv_freq = ROPE_THETA ** (-jnp.arange(half, dtype=F32) / half)
    ang = pos.astype(F32)[:, None] * inv_freq[None, :]
    cos, sin = jnp.cos(ang), jnp.sin(ang)
    return jnp.concatenate([cos, cos], axis=1), jnp.concatenate([-sin, sin], axis=1)


def kernel(x_prompt, x_sample, cache_k, cache_v, state_conv, page_table, c_prompt, c_sample, ada_w, ada_b, ln_g, ln_b, attn_w_qkv, attn_w_o, conv_w_in, conv_w, conv_w_out, router_w, router_b, exp_w_gate, exp_w_up, exp_w_down, shared_w_gate, shared_w_up, shared_w_down):
    n_p, seq, d = x_prompt.shape
    n_s, dec_seq, _ = x_sample.shape
    n_pool, n_moba_layers = cache_k.shape[:2]
    n_pages = page_table.shape[1]
    past_len = n_pages * PAGE_SIZE
    n_past_blocks = past_len // MOBA_BLOCK
    assert past_len % MOBA_BLOCK == 0 and (past_len + dec_seq - 1) // MOBA_BLOCK == n_past_blocks
    assert n_past_blocks >= MOBA_TOPK and n_p + n_s <= N_COND_ROWS and dec_seq <= 8
    t_p, t_s = n_p * seq, n_s * dec_seq

    c_all = jnp.zeros((N_COND_ROWS, d), F32).at[:n_p].set(c_prompt).at[n_p:n_p + n_s].set(c_sample)
    mods = ada_modulation_all(c_all, ada_w.reshape(2 * DEPTH, d, 3 * d), ada_b.reshape(2 * DEPTH, 1, 3 * d))

    def mod_of(i, j):
        m = mods[2 * i + j].reshape(N_COND_ROWS, 3, d)
        m_p = m[:n_p][:, None]
        m_s = jnp.repeat(m[n_p:n_p + n_s], dec_seq, axis=0)[None]
        return [(m_p[:, :, c], m_s[:, :, c]) for c in range(3)]

    cos_p, sin_p = _rope_tables(jnp.arange(seq))
    cos_s, sin_s = _rope_tables(past_len + jnp.tile(jnp.arange(dec_seq), n_s))
    cache_k2 = cache_k.reshape(n_pool, n_moba_layers, PAGE_SIZE, d)
    cache_v2 = cache_v.reshape(n_pool, n_moba_layers, PAGE_SIZE, d)

    x_p = x_prompt
    x_s = x_sample.reshape(1, t_s, d)
    k_p_rows, v_p_rows, k_s_rows, v_s_rows, conv_p_rows, conv_s_rows = [], [], [], [], [], []
    for i in range(DEPTH):
        li = i // 2
        (sh_p, sh_s), (sc_p, sc_s), (g_p, g_s) = mod_of(i, 0)
        (sh2_p, sh2_s), (sc2_p, sc2_s), (g2_p, g2_s) = mod_of(i, 1)
        if i % 2 == 0:
            qkv_p = modulated_projection(x_p, sc_p, sh_p, attn_w_qkv[li], cos_p, sin_p, rope=True, tm=1024)
            qkv_s = modulated_projection(x_s, sc_s, sh_s, attn_w_qkv[li], cos_s, sin_s, rope=True, tm=t_s)
            a_p = moba_prompt_attention(qkv_p)
            q_s, k_s, v_s = (qkv_s[c].reshape(n_s, dec_seq, d) for c in range(3))
            kmean = paged_block_means(cache_k2, page_table, li).reshape(n_s, n_past_blocks, d)
            q_pad = jnp.zeros((n_s, 8, d), F32).at[:, :dec_seq].set(q_s)
            sel = sample_block_select(q_pad, kmean)[:, :, :dec_seq, :MOBA_TOPK]
            sel = sel.transpose(0, 2, 1, 3)
            ppb = MOBA_BLOCK // PAGE_SIZE
            logical = (sel[..., None] * ppb + jnp.arange(ppb)).reshape(n_s, dec_seq, N_HEADS, MOBA_TOPK * ppb)
            phys = page_table[jnp.arange(n_s)[:, None, None, None], logical]
            a_s = moba_sample_attention(q_s, k_s, v_s, cache_k2, cache_v2, phys, li).reshape(1, t_s, d)
            k_p_rows.append(qkv_p[1])
            v_p_rows.append(qkv_p[2])
            k_s_rows.append(k_s)
            v_s_rows.append(v_s)
            w_out = attn_w_o[li]
        else:
            bcv_p = modulated_projection(x_p, sc_p, sh_p, conv_w_in[li], cos_p, sin_p, rope=False, tm=1024)
            bcv_s = modulated_projection(x_s, sc_s, sh_s, conv_w_in[li], cos_s, sin_s, rope=False, tm=t_s)
            a_p, st_p = conv_gate(bcv_p, jnp.zeros((n_p, CONV_WIDTH - 1, d), F32), conv_w[li], tl=512)
            a_s, st_s = conv_gate(bcv_s.reshape(3, n_s, dec_seq, d), state_conv[:, li], conv_w[li], tl=dec_seq)
            a_s = a_s.reshape(1, t_s, d)
            conv_p_rows.append(st_p)
            conv_s_rows.append(st_s)
            w_out = conv_w_out[li]
        g1, b1 = ln_g[i, 0][None], ln_b[i, 0][None]
        x_p, h_p, score_p = out_projection_norm(a_p, w_out, x_p, g_p, g1, b1, sc2_p, sh2_p, router_w[i], tm=512)
        x_s, h_s, score_s = out_projection_norm(a_s, w_out, x_s, g_s, g1, b1, sc2_s, sh2_s, router_w[i], tm=t_s)

        h_all = jnp.concatenate([h_p.reshape(t_p, d), h_s.reshape(t_s, d)], axis=0)
        scores = jnp.concatenate([score_p.reshape(t_p, N_EXPERTS), score_s.reshape(t_s, N_EXPERTS)], axis=0)
        idx, gate = _route(scores, router_b[i])
        row_tok, row_gate, block_expert, n_used, pos = _dispatch(idx, gate)
        ys = routed_expert_rows(block_expert, n_used, h_all[row_tok], row_gate[:, None],
                                exp_w_gate[i], exp_w_up[i], exp_w_down[i])
        routed = ys[pos].sum(axis=1)
        g2, b2 = ln_g[i, 1][None], ln_b[i, 1][None]
        x_p = shared_expert_norm(h_p, x_p, routed[:t_p].reshape(n_p, seq, d), g2_p, g2, b2,
                                 shared_w_gate[i], shared_w_up[i], shared_w_down[i], tm=256)
        x_s = shared_expert_norm(h_s, x_s, routed[t_p:].reshape(1, t_s, d), g2_s, g2, b2,
                                 shared_w_gate[i], shared_w_up[i], shared_w_down[i], tm=t_s)

    heads = lambda rows, n, l: jnp.stack(rows, axis=1).reshape(n, len(rows), l, N_HEADS, HEAD_DIM)
    return (x_p, x_s.reshape(n_s, dec_seq, d),
            heads(k_p_rows, n_p, seq), heads(v_p_rows, n_p, seq), jnp.stack(conv_p_rows, axis=1),
            heads(k_s_rows, n_s, dec_seq), heads(v_s_rows, n_s, dec_seq), jnp.stack(conv_s_rows, axis=1))
```

```python
import functools

import jax
import jax.numpy as jnp
from jax import lax
from jax.experimental import pallas as pl
from jax.experimental.pallas import tpu as pltpu

D_MODEL = 2048
DEPTH = 4
PAGE_SIZE = 128
N_HEADS = 16
HEAD_DIM = D_MODEL // N_HEADS
ROPE_THETA = 10000.0
MOBA_BLOCK = 256
MOBA_TOPK = 3
CONV_WIDTH = 3
N_EXPERTS = 64
TOP_K = 8
N_GROUPS = 8
TOPK_GROUPS = 4
D_EXPERT = 512
ROUTED_SCALE = 2.5
DEEPNORM_ALPHA = (2 * DEPTH) ** 0.25
LN_EPS = 1e-5

F32 = jnp.float32
BF16 = jnp.bfloat16
NEG_INF = float("-inf")

VMEM_LIMIT_BYTES = 56 * 1024 * 1024
MOE_ROW_BLOCK = 256
N_COND_ROWS = 16


def _params(n_axes):
    return pltpu.CompilerParams(dimension_semantics=("arbitrary",) * n_axes,
                                vmem_limit_bytes=VMEM_LIMIT_BYTES)


def _silu(x):
    return x / (1.0 + jnp.exp(-x))


def _split_bf16(x):
    hi = x.astype(BF16)
    lo = (x - hi.astype(F32)).astype(BF16)
    return hi, lo


def _dot(a, b):
    return jnp.dot(a, b, preferred_element_type=F32)


def _dot3(a, b):
    a_hi, a_lo = _split_bf16(a)
    b_hi, b_lo = _split_bf16(b)
    return _dot(a_hi, b_hi) + _dot(a_lo, b_hi) + _dot(a_hi, b_lo)


def _ada_kernel(c_ref, w_ref, b_ref, o_ref):
    o_ref[0] = _dot3(_silu(c_ref[...]), w_ref[0]) + b_ref[0]


def ada_modulation_all(c_all, ada_w, ada_b):
    n_sub = ada_w.shape[0]
    tn = 768
    return pl.pallas_call(
        _ada_kernel,
        grid=(n_sub, 3 * D_MODEL // tn),
        in_specs=[
            pl.BlockSpec((N_COND_ROWS, D_MODEL), lambda s, j: (0, 0)),
            pl.BlockSpec((1, D_MODEL, tn), lambda s, j: (s, 0, j)),
            pl.BlockSpec((1, 1, tn), lambda s, j: (s, 0, j)),
        ],
        out_specs=pl.BlockSpec((1, N_COND_ROWS, tn), lambda s, j: (s, 0, j)),
        out_shape=jax.ShapeDtypeStruct((n_sub, N_COND_ROWS, 3 * D_MODEL), F32),
        compiler_params=_params(2),
        name="ada_modulation",
    )(c_all, ada_w, ada_b)


def _proj_kernel(x_ref, sc_ref, sh_ref, w_ref, cos_ref, sin_ref, o_ref, hb_ref, *, rope, tn):
    c = pl.program_id(2)
    j = pl.program_id(3)

    @pl.when((c == 0) & (j == 0))
    def _():
        hb_ref[...] = (x_ref[0] * (1.0 + sc_ref[0]) + sh_ref[0]).astype(BF16)

    acc = _dot(hb_ref[...], w_ref[...].astype(BF16))
    if rope:
        @pl.when(c < 2)
        def _():
            cos = cos_ref[...]
            sin = sin_ref[...]
            for h in range(tn // HEAD_DIM):
                seg = acc[:, h * HEAD_DIM:(h + 1) * HEAD_DIM]
                o_ref[0, 0, :, h * HEAD_DIM:(h + 1) * HEAD_DIM] = (
                    seg * cos + pltpu.roll(seg, HEAD_DIM // 2, axis=1) * sin)

        @pl.when(c == 2)
        def _():
            o_ref[0, 0] = acc
    else:
        o_ref[0, 0] = acc


def modulated_projection(x, sc, sh, w, cos, sin, *, rope, tm):
    nb, seq, _ = x.shape
    r = sc.shape[1]
    tn = 512
    nj = D_MODEL // tn
    return pl.pallas_call(
        functools.partial(_proj_kernel, rope=rope, tn=tn),
        grid=(nb, seq // tm, 3, nj),
        in_specs=[
            pl.BlockSpec((1, tm, D_MODEL), lambda b, m, c, j: (b, m, 0)),
            pl.BlockSpec((1, r, D_MODEL), lambda b, m, c, j: (b, 0, 0)),
            pl.BlockSpec((1, r, D_MODEL), lambda b, m, c, j: (b, 0, 0)),
            pl.BlockSpec((D_MODEL, tn), lambda b, m, c, j: (0, c * nj + j)),
            pl.BlockSpec((tm, HEAD_DIM), lambda b, m, c, j: (m, 0)),
            pl.BlockSpec((tm, HEAD_DIM), lambda b, m, c, j: (m, 0)),
        ],
        out_specs=pl.BlockSpec((1, 1, tm, tn), lambda b, m, c, j: (c, b, m, j)),
        out_shape=jax.ShapeDtypeStruct((3, nb, seq, D_MODEL), F32),
        scratch_shapes=[pltpu.VMEM((tm, D_MODEL), BF16)],
        compiler_params=_params(4),
        name="modulated_projection",
    )(x, sc, sh, w, cos, sin)


KEY_GROUP = 4


def _moba_prompt_kernel(q_ref, k_ref, v_ref, o_ref, knew_ref, vnew_ref, kb_ref, vt_ref, kmean_ref, bias_ref,
                        sem, *, n_blocks):
    b = pl.program_id(0)
    h = pl.program_id(1)
    qi = pl.program_id(2)
    blk_rows = MOBA_BLOCK
    grp_rows = KEY_GROUP * MOBA_BLOCK
    k_out = pltpu.make_async_copy(k_ref.at[0, 0], knew_ref.at[b, 0, :, h, :], sem.at[0])
    v_out = pltpu.make_async_copy(v_ref.at[0, 0], vnew_ref.at[b, 0, :, h, :], sem.at[1])

    @pl.when(qi == 0)
    def _():
        k_out.start()
        v_out.start()
        for n in range(n_blocks):
            g, j = divmod(n, KEY_GROUP)
            k_blk = k_ref[0, 0, n * blk_rows:(n + 1) * blk_rows, :]
            kb_ref[g, j * blk_rows:(j + 1) * blk_rows, :] = k_blk.astype(BF16)
            kmean_ref[n:n + 1, :] = jnp.sum(k_blk, axis=0, keepdims=True) * (1.0 / MOBA_BLOCK)
            vt_ref[g, :, j * blk_rows:(j + 1) * blk_rows] = (
                v_ref[0, 0, n * blk_rows:(n + 1) * blk_rows, :].T.astype(BF16))

    @pl.when(qi == n_blocks - 1)
    def _():
        k_out.wait()
        v_out.wait()

    q = q_ref[0, 0]
    q_t = q.T
    sc = _dot3(kmean_ref[...], q_t)
    blk = lax.broadcasted_iota(jnp.int32, sc.shape, 0).astype(F32)
    sc = jnp.where(blk < qi.astype(F32), sc, NEG_INF)
    bias = jnp.full(sc.shape, NEG_INF, F32)
    for _ in range(MOBA_TOPK):
        best = jnp.max(sc, axis=0, keepdims=True)
        first = jnp.min(jnp.where(sc == best, blk, float(n_blocks)), axis=0, keepdims=True)
        pick = (blk == first) & (best > NEG_INF)
        bias = jnp.where(pick, 0.0, bias)
        sc = jnp.where(pick, NEG_INF, sc)
    bias_ref[...] = bias

    qs_t = (q_t * (HEAD_DIM ** -0.5)).astype(BF16)
    own_group = qi // KEY_GROUP
    causal = jnp.where(lax.broadcasted_iota(jnp.int32, (blk_rows, blk_rows), 0)
                       <= lax.broadcasted_iota(jnp.int32, (blk_rows, blk_rows), 1), 0.0, NEG_INF)

    def scores(g, own):
        s = _dot(kb_ref[g], qs_t)
        pieces = []
        for j in range(KEY_GROUP):
            n = g * KEY_GROUP + j
            mask = bias_ref[pl.ds(n, 1), :]
            if own:
                mask = jnp.where(n == qi, causal, jnp.broadcast_to(mask, causal.shape))
            pieces.append(s[j * blk_rows:(j + 1) * blk_rows] + mask)
        return pieces

    def softmax_terms(g, pieces, m_prev):
        m_new = m_prev
        for piece in pieces:
            col_max = jnp.max(piece, axis=0, keepdims=True)
            m_new = col_max if m_new is None else jnp.maximum(m_new, col_max)
        probs = [jnp.exp(piece - m_new) for piece in pieces]
        l_new = sum(jnp.sum(p, axis=0, keepdims=True) for p in probs)
        pv = _dot(vt_ref[g], jnp.concatenate(probs, axis=0).astype(BF16))
        return m_new, l_new, pv

    carry0 = softmax_terms(own_group, scores(own_group, own=True), None)

    def body(g, carry):
        m, l, acc = carry
        m_new, l_new, pv = softmax_terms(g, scores(g, own=False), m)
        alpha = jnp.exp(m - m_new)
        return m_new, alpha * l + l_new, alpha * acc + pv

    _, l, acc = lax.fori_loop(0, own_group, body, carry0)
    o_ref[0] = (acc / l).T


def moba_prompt_attention(qkv):
    _, nb, seq, _ = qkv.shape
    n_blocks = seq // MOBA_BLOCK
    assert n_blocks % KEY_GROUP == 0
    n_groups = n_blocks // KEY_GROUP
    cache_shape = jax.ShapeDtypeStruct((nb, 1, seq, N_HEADS, HEAD_DIM), F32)
    return pl.pallas_call(
        functools.partial(_moba_prompt_kernel, n_blocks=n_blocks),
        grid=(nb, N_HEADS, n_blocks),
        in_specs=[
            pl.BlockSpec((1, 1, MOBA_BLOCK, HEAD_DIM), lambda b, h, i: (0, b, i, h)),
            pl.BlockSpec((1, 1, seq, HEAD_DIM), lambda b, h, i: (1, b, 0, h)),
            pl.BlockSpec((1, 1, seq, HEAD_DIM), lambda b, h, i: (2, b, 0, h)),
        ],
        out_specs=[pl.BlockSpec((1, MOBA_BLOCK, HEAD_DIM), lambda b, h, i: (b, i, h)),
                   pl.BlockSpec(memory_space=pl.ANY), pl.BlockSpec(memory_space=pl.ANY)],
        out_shape=[jax.ShapeDtypeStruct((nb, seq, D_MODEL), F32), cache_shape, cache_shape],
        scratch_shapes=[
            pltpu.VMEM((n_groups, KEY_GROUP * MOBA_BLOCK, HEAD_DIM), BF16),
            pltpu.VMEM((n_groups, HEAD_DIM, KEY_GROUP * MOBA_BLOCK), BF16),
            pltpu.VMEM((n_blocks, HEAD_DIM), F32),
            pltpu.VMEM((n_blocks, MOBA_BLOCK), F32),
            pltpu.SemaphoreType.DMA((2,)),
        ],
        compiler_params=_params(3),
        name="moba_prompt_attention",
    )(qkv, qkv, qkv)


PAGES_PER_BLOCK = MOBA_BLOCK // PAGE_SIZE
KMEAN_BLOCKS_PER_STEP = 2


def _kmean_kernel(pt_ref, *refs):
    del pt_ref
    page_refs, o_ref = refs[:-1], refs[-1]
    for n in range(KMEAN_BLOCKS_PER_STEP):
        tot = sum(jnp.sum(page_refs[n * PAGES_PER_BLOCK + p][0, 0], axis=0) for p in range(PAGES_PER_BLOCK))
        o_ref[0, n] = tot * (1.0 / MOBA_BLOCK)


def paged_block_means(cache_k, page_table, li):
    nb, n_pages = page_table.shape
    n_blocks = n_pages // PAGES_PER_BLOCK
    pages_per_step = KMEAN_BLOCKS_PER_STEP * PAGES_PER_BLOCK
    assert n_blocks % KMEAN_BLOCKS_PER_STEP == 0
    page_spec = lambda p: pl.BlockSpec(
        (1, 1, PAGE_SIZE, N_HEADS, HEAD_DIM), lambda b, n, pt: (pt[b, n * pages_per_step + p], li, 0, 0, 0))
    return pl.pallas_call(
        _kmean_kernel,
        grid_spec=pltpu.PrefetchScalarGridSpec(
            num_scalar_prefetch=1,
            grid=(nb, n_blocks // KMEAN_BLOCKS_PER_STEP),
            in_specs=[page_spec(p) for p in range(pages_per_step)],
            out_specs=pl.BlockSpec((1, KMEAN_BLOCKS_PER_STEP, N_HEADS, HEAD_DIM), lambda b, n, pt: (b, n, 0, 0)),
        ),
        out_shape=jax.ShapeDtypeStruct((nb, n_blocks, N_HEADS, HEAD_DIM), F32),
        compiler_params=_params(2),
        name="paged_block_means",
    )(page_table, *([cache_k] * pages_per_step))


def _sample_select_kernel(q_ref, km_ref, o_ref, *, n_blocks):
    lane = lax.broadcasted_iota(jnp.int32, (8, 128), 1)
    for h in range(N_HEADS):
        q = q_ref[0, :, h * HEAD_DIM:(h + 1) * HEAD_DIM]
        km = km_ref[0, :, h * HEAD_DIM:(h + 1) * HEAD_DIM]
        q_hi, q_lo = _split_bf16(q)
        k_hi, k_lo = _split_bf16(km)
        nt = (((1,), (1,)), ((), ()))
        sc = (lax.dot_general(q_hi, k_hi, nt, preferred_element_type=F32)
              + lax.dot_general(q_lo, k_hi, nt, preferred_element_type=F32)
              + lax.dot_general(q_hi, k_lo, nt, preferred_element_type=F32))
        blk = lax.broadcasted_iota(jnp.int32, sc.shape, 1).astype(F32)
        out = jnp.zeros((8, 128), F32)
        for t in range(MOBA_TOPK):
            best = jnp.max(sc, axis=1, keepdims=True)
            first = jnp.min(jnp.where(sc == best, blk, float(n_blocks)), axis=1, keepdims=True)
            out = jnp.where(lane == t, first, out)
            sc = jnp.where(blk == first, NEG_INF, sc)
        o_ref[0, h] = out.astype(jnp.int32)


def sample_block_select(q_pad, kmean):
    nb, n_blocks, _ = kmean.shape
    return pl.pallas_call(
        functools.partial(_sample_select_kernel, n_blocks=n_blocks),
        grid=(nb,),
        in_specs=[
            pl.BlockSpec((1, 8, D_MODEL), lambda b: (b, 0, 0)),
            pl.BlockSpec((1, n_blocks, D_MODEL), lambda b: (b, 0, 0)),
        ],
        out_specs=pl.BlockSpec((1, N_HEADS, 8, 128), lambda b: (b, 0, 0, 0)),
        out_shape=jax.ShapeDtypeStruct((nb, N_HEADS, 8, 128), jnp.int32),
        compiler_params=_params(1),
        name="sample_block_select",
    )(q_pad, kmean)


def _sample_attn_kernel(phys_ref, q_ref, ko_ref, vo_ref, ck_ref, cv_ref, o_ref, kbuf, vbuf, sem,
                        *, n_pages, dec_seq, li):
    b = pl.program_id(0)
    h = pl.program_id(1)
    step = b * N_HEADS + h
    slot = step % 2

    def page_copies(step_, slot_):
        b_, h_ = step_ // N_HEADS, step_ % N_HEADS
        copies = []
        for l in range(dec_seq):
            for p in range(n_pages):
                page = phys_ref[((b_ * dec_seq + l) * N_HEADS + h_) * n_pages + p]
                copies.append(pltpu.make_async_copy(ck_ref.at[page, li, :, h_, :], kbuf.at[slot_, l, p], sem.at[0, slot_]))
                copies.append(pltpu.make_async_copy(cv_ref.at[page, li, :, h_, :], vbuf.at[slot_, l, p], sem.at[1, slot_]))
        return copies

    @pl.when(step == 0)
    def _():
        for c in page_copies(step, slot):
            c.start()

    @pl.when(step + 1 < pl.num_programs(0) * N_HEADS)
    def _():
        for c in page_copies(step + 1, 1 - slot):
            c.start()

    for c in page_copies(step, slot):
        c.wait()

    nt = (((1,), (1,)), ((), ()))
    k_own = ko_ref[0]
    row = lax.broadcasted_iota(jnp.int32, (dec_seq, 1), 0)
    for l in range(dec_seq):
        qs = q_ref[0, l:l + 1, :] * (HEAD_DIM ** -0.5)
        q8 = jnp.broadcast_to(qs, (8, HEAD_DIM)).astype(BF16)
        s_sel = [lax.dot_general(q8, kbuf[slot, l, p].astype(BF16), nt, preferred_element_type=F32)[0:1]
                 for p in range(n_pages)]
        s_own = jnp.sum(k_own * qs, axis=1, keepdims=True)
        s_own = jnp.where(row <= l, s_own, NEG_INF)
        m = jnp.max(s_own, axis=0, keepdims=True)
        for s in s_sel:
            m = jnp.maximum(m, jnp.max(s, axis=1, keepdims=True))
        p_own = jnp.exp(s_own - m)
        denom = jnp.sum(p_own, axis=0, keepdims=True)
        acc = jnp.sum(p_own * vo_ref[0], axis=0, keepdims=True)
        for p in range(n_pages):
            pr = jnp.exp(s_sel[p] - m)
            denom = denom + jnp.sum(pr, axis=1, keepdims=True)
            p8 = jnp.broadcast_to(pr, (8, PAGE_SIZE)).astype(BF16)
            acc = acc + _dot(p8, vbuf[slot, l, p].astype(BF16))[0:1]
        o_ref[0, l:l + 1, :] = acc / denom


def moba_sample_attention(q, k_new, v_new, cache_k, cache_v, phys, li):
    nb, dec_seq, _ = q.shape
    n_pages = phys.shape[-1]
    own_spec = pl.BlockSpec((1, dec_seq, HEAD_DIM), lambda b, h, ph: (b, 0, h))
    buf = pltpu.VMEM((2, dec_seq, n_pages, PAGE_SIZE, HEAD_DIM), F32)
    return pl.pallas_call(
        functools.partial(_sample_attn_kernel, n_pages=n_pages, dec_seq=dec_seq, li=li),
        grid_spec=pltpu.PrefetchScalarGridSpec(
            num_scalar_prefetch=1,
            grid=(nb, N_HEADS),
            in_specs=[own_spec, own_spec, own_spec,
                      pl.BlockSpec(memory_space=pl.ANY), pl.BlockSpec(memory_space=pl.ANY)],
            out_specs=own_spec,
            scratch_shapes=[buf, buf, pltpu.SemaphoreType.DMA((2, 2))],
        ),
        out_shape=jax.ShapeDtypeStruct((nb, dec_seq, D_MODEL), F32),
        compiler_params=_params(2),
        name="moba_sample_attention",
    )(phys.reshape(-1), q, k_new, v_new, cache_k, cache_v)


CONV_PAD_ROWS = 8


def _conv_gate_kernel(*refs, tl, has_halo):
    if has_halo:
        b_ref, c_ref, v_ref, ch_ref, vh_ref, hist_ref, w_ref, z_ref, st_ref, ext_ref = refs
    else:
        b_ref, c_ref, v_ref, hist_ref, w_ref, z_ref, st_ref, ext_ref = refs
    l_idx = pl.program_id(1)
    u = c_ref[0, 0] * v_ref[0, 0]
    ext_ref[CONV_PAD_ROWS:CONV_PAD_ROWS + tl, :] = u
    n_hist = CONV_WIDTH - 1

    @pl.when(l_idx == 0)
    def _():
        ext_ref[CONV_PAD_ROWS - n_hist:CONV_PAD_ROWS, :] = hist_ref[0]

    if has_halo:
        @pl.when(l_idx > 0)
        def _():
            ext_ref[0:CONV_PAD_ROWS, :] = ch_ref[0, 0] * vh_ref[0, 0]

    w = w_ref[...]
    conv = u * w[2:3, :]
    conv = conv + ext_ref[CONV_PAD_ROWS - 1:CONV_PAD_ROWS - 1 + tl, :] * w[1:2, :]
    conv = conv + ext_ref[CONV_PAD_ROWS - 2:CONV_PAD_ROWS - 2 + tl, :] * w[0:1, :]
    z_ref[0] = b_ref[0, 0] * conv

    @pl.when(l_idx == pl.num_programs(1) - 1)
    def _():
        st_ref[0] = ext_ref[CONV_PAD_ROWS + tl - n_hist:CONV_PAD_ROWS + tl, :]


def conv_gate(bcv, hist, w_conv, *, tl):
    _, nb, seq, _ = bcv.shape
    n_hist = CONV_WIDTH - 1
    nl = seq // tl
    has_halo = nl > 1
    part = lambda c: pl.BlockSpec((1, 1, tl, D_MODEL), lambda b, l: (c, b, l, 0))
    halo = lambda c: pl.BlockSpec(
        (1, 1, CONV_PAD_ROWS, D_MODEL),
        lambda b, l: (c, b, jnp.maximum(l * (tl // CONV_PAD_ROWS) - 1, 0), 0))
    in_specs = [part(0), part(1), part(2)]
    args = [bcv, bcv, bcv]
    if has_halo:
        in_specs += [halo(1), halo(2)]
        args += [bcv, bcv]
    in_specs += [pl.BlockSpec((1, n_hist, D_MODEL), lambda b, l: (b, 0, 0)),
                 pl.BlockSpec((CONV_WIDTH, D_MODEL), lambda b, l: (0, 0))]
    args += [hist, w_conv]
    return pl.pallas_call(
        functools.partial(_conv_gate_kernel, tl=tl, has_halo=has_halo),
        grid=(nb, nl),
        in_specs=in_specs,
        out_specs=[pl.BlockSpec((1, tl, D_MODEL), lambda b, l: (b, l, 0)),
                   pl.BlockSpec((1, n_hist, D_MODEL), lambda b, l: (b, 0, 0))],
        out_shape=[jax.ShapeDtypeStruct((nb, seq, D_MODEL), F32),
                   jax.ShapeDtypeStruct((nb, n_hist, D_MODEL), F32)],
        scratch_shapes=[pltpu.VMEM((CONV_PAD_ROWS + tl, D_MODEL), F32)],
        compiler_params=_params(2),
        name="conv_gate",
    )(*args)


def _layer_norm(z, g, b):
    mu = jnp.mean(z, axis=-1, keepdims=True)
    d = z - mu
    var = jnp.mean(d * d, axis=-1, keepdims=True)
    return d * lax.rsqrt(var + LN_EPS) * g + b


def _out_norm_kernel(a_ref, w_ref, x_ref, gate_ref, g_ref, b_ref, sc_ref, sh_ref, wr_ref,
                     xo_ref, h_ref, score_ref, acc_ref):
    k = pl.program_id(2)

    @pl.when(k == 0)
    def _():
        acc_ref[...] = jnp.zeros_like(acc_ref)

    acc_ref[...] += _dot(a_ref[0].astype(BF16), w_ref[...].astype(BF16))

    @pl.when(k == pl.num_programs(2) - 1)
    def _():
        xn = _layer_norm(DEEPNORM_ALPHA * x_ref[0] + gate_ref[0] * acc_ref[...], g_ref[...], b_ref[...])
        xo_ref[0] = xn
        h = xn * (1.0 + sc_ref[0]) + sh_ref[0]
        h_ref[0] = h.astype(BF16)
        logits = _dot3(h, wr_ref[...])
        score_ref[0] = 1.0 / (1.0 + jnp.exp(-logits))


def out_projection_norm(a, w, x, gate, ln_g, ln_b, sc_next, sh_next, w_router, *, tm):
    nb, seq, kdim = a.shape
    r = gate.shape[1]
    tk = 512
    row = pl.BlockSpec((1, tm, D_MODEL), lambda b, m, k: (b, m, 0))
    mod = pl.BlockSpec((1, r, D_MODEL), lambda b, m, k: (b, 0, 0))
    vec = pl.BlockSpec((1, D_MODEL), lambda b, m, k: (0, 0))
    n_lanes = w_router.shape[1]
    return pl.pallas_call(
        _out_norm_kernel,
        grid=(nb, seq // tm, kdim // tk),
        in_specs=[
            pl.BlockSpec((1, tm, tk), lambda b, m, k: (b, m, k)),
            pl.BlockSpec((tk, D_MODEL), lambda b, m, k: (k, 0)),
            row, mod, vec, vec, mod, mod,
            pl.BlockSpec((D_MODEL, n_lanes), lambda b, m, k: (0, 0)),
        ],
        out_specs=[row, row, pl.BlockSpec((1, tm, n_lanes), lambda b, m, k: (b, m, 0))],
        out_shape=[jax.ShapeDtypeStruct((nb, seq, D_MODEL), F32),
                   jax.ShapeDtypeStruct((nb, seq, D_MODEL), BF16),
                   jax.ShapeDtypeStruct((nb, seq, n_lanes), F32)],
        scratch_shapes=[pltpu.VMEM((tm, D_MODEL), F32)],
        compiler_params=_params(3),
        name="out_projection_norm",
    )(a, w, x, gate, ln_g, ln_b, sc_next, sh_next, w_router)


def _swiglu(x_bf16, wg, wu, wd):
    g = _dot(x_bf16, wg)
    u = _dot(x_bf16, wu)
    return _dot((_silu(g) * u).astype(BF16), wd)


def _experts_kernel(be_ref, nb_ref, xs_ref, wg_ref, wu_ref, wd_ref, ys_ref, wgb, wub, wdb):
    i = pl.program_id(0)
    active = i < nb_ref[0]
    new_expert = (i == 0) | (be_ref[i] != be_ref[jnp.maximum(i - 1, 0)])

    @pl.when(active & new_expert)
    def _():
        wgb[...] = wg_ref[0].astype(BF16)
        wub[...] = wu_ref[0].astype(BF16)
        wdb[...] = wd_ref[0].astype(BF16)

    @pl.when(active)
    def _():
        ys_ref[...] = _swiglu(xs_ref[...].astype(BF16), wgb[...], wub[...], wdb[...])

    @pl.when(jnp.logical_not(active))
    def _():
        ys_ref[...] = jnp.zeros_like(ys_ref)


def routed_expert_rows(block_expert, n_used_blocks, xs, w_gate, w_up, w_down):
    n_rows = xs.shape[0]
    bm = MOE_ROW_BLOCK
    wspec = lambda shape: pl.BlockSpec((1,) + shape, lambda i, be, nb: (be[i], 0, 0))
    return pl.pallas_call(
        _experts_kernel,
        grid_spec=pltpu.PrefetchScalarGridSpec(
            num_scalar_prefetch=2,
            grid=(n_rows // bm,),
            in_specs=[
                pl.BlockSpec((bm, D_MODEL), lambda i, be, nb: (i, 0)),
                wspec((D_MODEL, D_EXPERT)), wspec((D_MODEL, D_EXPERT)), wspec((D_EXPERT, D_MODEL)),
            ],
            out_specs=pl.BlockSpec((bm, D_MODEL), lambda i, be, nb: (i, 0)),
            scratch_shapes=[pltpu.VMEM((D_MODEL, D_EXPERT), BF16),
                            pltpu.VMEM((D_MODEL, D_EXPERT), BF16),
                            pltpu.VMEM((D_EXPERT, D_MODEL), BF16)],
        ),
        out_shape=jax.ShapeDtypeStruct((n_rows, D_MODEL), F32),
        compiler_params=_params(1),
        name="routed_experts",
    )(block_expert, n_used_blocks, xs, w_gate, w_up, w_down)


def _shared_norm_kernel(h_ref, x_ref, r_ref, gate_ref, g_ref, b_ref, wg_ref, wu_ref, wd_ref,
                        xo_ref, wgb, wub, wdb):
    @pl.when((pl.program_id(0) == 0) & (pl.program_id(1) == 0))
    def _():
        wgb[...] = wg_ref[...].astype(BF16)
        wub[...] = wu_ref[...].astype(BF16)
        wdb[...] = wd_ref[...].astype(BF16)

    f = r_ref[0] + _swiglu(h_ref[0].astype(BF16), wgb[...], wub[...], wdb[...])
    xo_ref[0] = _layer_norm(DEEPNORM_ALPHA * x_ref[0] + gate_ref[0] * f, g_ref[...], b_ref[...])


def shared_expert_norm(h, x, routed, gate, ln_g, ln_b, ws_gate, ws_up, ws_down, *, tm):
    nb, seq, _ = h.shape
    r = gate.shape[1]
    d_shared = ws_gate.shape[1]
    row = pl.BlockSpec((1, tm, D_MODEL), lambda b, m: (b, m, 0))
    vec = pl.BlockSpec((1, D_MODEL), lambda b, m: (0, 0))
    whole = lambda shape: pl.BlockSpec(shape, lambda b, m: (0, 0), pipeline_mode=pl.Buffered(1))
    return pl.pallas_call(
        _shared_norm_kernel,
        grid=(nb, seq // tm),
        in_specs=[row, row, row, pl.BlockSpec((1, r, D_MODEL), lambda b, m: (b, 0, 0)), vec, vec,
                  whole((D_MODEL, d_shared)), whole((D_MODEL, d_shared)), whole((d_shared, D_MODEL))],
        out_specs=row,
        out_shape=jax.ShapeDtypeStruct((nb, seq, D_MODEL), F32),
        scratch_shapes=[pltpu.VMEM((D_MODEL, d_shared), BF16),
                        pltpu.VMEM((D_MODEL, d_shared), BF16),
                        pltpu.VMEM((d_shared, D_MODEL), BF16)],
        compiler_params=_params(2),
        name="shared_expert_norm",
    )(h, x, routed, gate, ln_g, ln_b, ws_gate, ws_up, ws_down)


ROUTE_TILE = 256
GROUP_SIZE = N_EXPERTS // N_GROUPS


def _first_argmax(x, ids, n):
    best = jnp.max(x, axis=0, keepdims=True)
    first = jnp.min(jnp.where(x == best, ids, float(n)), axis=0, keepdims=True)
    return best, first


def _route_kernel(s_ref, b_ref, idx_ref, gate_ref, rank_ref, cnt_ref, carry_ref, *, n_valid):
    i = pl.program_id(0)
    tm = ROUTE_TILE

    @pl.when(i == 0)
    def _():
        carry_ref[...] = jnp.zeros_like(carry_ref)

    raw = s_ref[...].T[:N_EXPERTS]
    biased = raw + b_ref[...]
    member = lax.broadcasted_iota(jnp.int32, (GROUP_SIZE, tm), 0).astype(F32)
    group_id = lax.broadcasted_iota(jnp.int32, (N_GROUPS, tm), 0).astype(F32)
    gs = jnp.zeros((N_GROUPS, tm), F32)
    for g in range(N_GROUPS):
        x = biased[g * GROUP_SIZE:(g + 1) * GROUP_SIZE]
        m1, i1 = _first_argmax(x, member, GROUP_SIZE)
        m2 = jnp.max(jnp.where(member == i1, NEG_INF, x), axis=0, keepdims=True)
        gs = jnp.where(group_id == g, m1 + m2, gs)
    keep = jnp.zeros((N_GROUPS, tm), F32)
    for _ in range(TOPK_GROUPS):
        _, first = _first_argmax(gs, group_id, N_GROUPS)
        pick = group_id == first
        keep = jnp.where(pick, 1.0, keep)
        gs = jnp.where(pick, NEG_INF, gs)
    masked = jnp.concatenate(
        [jnp.where(keep[g:g + 1] > 0.0, biased[g * GROUP_SIZE:(g + 1) * GROUP_SIZE], NEG_INF)
         for g in range(N_GROUPS)], axis=0)
    expert_id = lax.broadcasted_iota(jnp.int32, (N_EXPERTS, tm), 0).astype(F32)
    slot = lax.broadcasted_iota(jnp.int32, (TOP_K, tm), 0)
    idx = jnp.zeros((TOP_K, tm), F32)
    w = jnp.zeros((TOP_K, tm), F32)
    chosen = jnp.zeros((N_EXPERTS, tm), F32)
    picks = []
    for k in range(TOP_K):
        _, first = _first_argmax(masked, expert_id, N_EXPERTS)
        pick = expert_id == first
        picks.append(pick)
        idx = jnp.where(slot == k, first, idx)
        w = jnp.where(slot == k, jnp.sum(jnp.where(pick, raw, 0.0), axis=0, keepdims=True), w)
        chosen = jnp.where(pick, 1.0, chosen)
        masked = jnp.where(pick, NEG_INF, masked)
    gate_ref[...] = w / jnp.sum(w, axis=0, keepdims=True) * ROUTED_SCALE
    idx_ref[...] = idx.astype(jnp.int32)

    tok = lax.broadcasted_iota(jnp.int32, (1, tm), 1) + i * tm
    chosen = jnp.where(tok < n_valid, chosen, 0.0)
    earlier = (lax.broadcasted_iota(jnp.int32, (tm, tm), 0)
               < lax.broadcasted_iota(jnp.int32, (tm, tm), 1)).astype(BF16)
    before = _dot(chosen.astype(BF16), earlier) + carry_ref[...]
    rank = jnp.zeros((TOP_K, tm), F32)
    for k in range(TOP_K):
        rank = jnp.where(slot == k, jnp.sum(jnp.where(picks[k], before, 0.0), axis=0, keepdims=True), rank)
    rank_ref[...] = rank.astype(jnp.int32)
    carry_ref[...] += jnp.sum(chosen, axis=1, keepdims=True)

    @pl.when(i == pl.num_programs(0) - 1)
    def _():
        cnt_ref[...] = jnp.broadcast_to(carry_ref[...], cnt_ref.shape).astype(jnp.int32)


def route_tokens(scores, b_router, n_valid):
    n_pad = scores.shape[0]
    col = pl.BlockSpec((TOP_K, ROUTE_TILE), lambda i: (0, i))
    return pl.pallas_call(
        functools.partial(_route_kernel, n_valid=n_valid),
        grid=(n_pad // ROUTE_TILE,),
        in_specs=[pl.BlockSpec((ROUTE_TILE, scores.shape[1]), lambda i: (i, 0)),
                  pl.BlockSpec((N_EXPERTS, 1), lambda i: (0, 0))],
        out_specs=[col, col, col, pl.BlockSpec((N_EXPERTS, 128), lambda i: (0, 0))],
        out_shape=[jax.ShapeDtypeStruct((TOP_K, n_pad), jnp.int32),
                   jax.ShapeDtypeStruct((TOP_K, n_pad), F32),
                   jax.ShapeDtypeStruct((TOP_K, n_pad), jnp.int32),
                   jax.ShapeDtypeStruct((N_EXPERTS, 128), jnp.int32)],
        scratch_shapes=[pltpu.VMEM((N_EXPERTS, 1), F32)],
        compiler_params=_params(1),
        name="route_tokens",
    )(scores, b_router.reshape(N_EXPERTS, 1))


def _dispatch(idx_t, rank_t, counts, n_tok):
    bm = MOE_ROW_BLOCK
    n_blocks = -(-(n_tok * TOP_K) // bm) + N_EXPERTS
    padded = (counts + bm - 1) // bm * bm
    pad_end = jnp.cumsum(padded)
    pad_start = pad_end - padded
    dest = (pad_start[idx_t] + rank_t)[:, :n_tok]
    tok = jnp.broadcast_to(jnp.arange(n_tok, dtype=jnp.int32)[None], dest.shape)
    row_tok = jnp.zeros((n_blocks * bm,), jnp.int32).at[dest.reshape(-1)].set(tok.reshape(-1))
    block_start = jnp.arange(n_blocks, dtype=jnp.int32) * bm
    block_expert = jnp.minimum(
        jnp.sum((pad_end[None, :] <= block_start[:, None]).astype(jnp.int32), axis=1), N_EXPERTS - 1)
    n_used = (pad_end[-1] // bm).astype(jnp.int32).reshape(1)
    return dest, row_tok, block_expert, n_used


def _rope_tables(pos):
    half = HEAD_DIM // 2
    inv_freq = ROPE_THETA ** (-jnp.arange(half, dtype=F32) / half)
    ang = pos.astype(F32)[:, None] * inv_freq[None, :]
    cos, sin = jnp.cos(ang), jnp.sin(ang)
    return jnp.concatenate([cos, cos], axis=1), jnp.concatenate([-sin, sin], axis=1)


def kernel(x_prompt, x_sample, cache_k, cache_v, state_conv, page_table, c_prompt, c_sample, ada_w, ada_b, ln_g, ln_b, attn_w_qkv, attn_w_o, conv_w_in, conv_w, conv_w_out, router_w, router_b, exp_w_gate, exp_w_up, exp_w_down, shared_w_gate, shared_w_up, shared_w_down):
    n_p, seq, d = x_prompt.shape
    n_s, dec_seq, _ = x_sample.shape
    n_pool, n_moba_layers = cache_k.shape[:2]
    n_pages = page_table.shape[1]
    past_len = n_pages * PAGE_SIZE
    n_past_blocks = past_len // MOBA_BLOCK
    assert past_len % MOBA_BLOCK == 0 and (past_len + dec_seq - 1) // MOBA_BLOCK == n_past_blocks
    assert n_past_blocks >= MOBA_TOPK and n_p + n_s <= N_COND_ROWS and dec_seq <= 8
    t_p, t_s = n_p * seq, n_s * dec_seq
    t_all = t_p + t_s
    t_pad = -(-t_all // ROUTE_TILE) * ROUTE_TILE

    c_all = jnp.zeros((N_COND_ROWS, d), F32).at[:n_p].set(c_prompt).at[n_p:n_p + n_s].set(c_sample)
    mods = ada_modulation_all(c_all, ada_w.reshape(2 * DEPTH, d, 3 * d), ada_b.reshape(2 * DEPTH, 1, 3 * d))

    def mod_of(i, j):
        m = mods[2 * i + j].reshape(N_COND_ROWS, 3, d)
        m_p = m[:n_p][:, None]
        m_s = jnp.repeat(m[n_p:n_p + n_s], dec_seq, axis=0)[None]
        return [(m_p[:, :, c], m_s[:, :, c]) for c in range(3)]

    cos_p, sin_p = _rope_tables(jnp.arange(seq))
    cos_s, sin_s = _rope_tables(past_len + jnp.tile(jnp.arange(dec_seq), n_s))

    x_p = x_prompt
    x_s = x_sample.reshape(1, t_s, d)
    k_p_rows, v_p_rows, k_s_rows, v_s_rows, conv_p_rows, conv_s_rows = [], [], [], [], [], []
    for i in range(DEPTH):
        li = i // 2
        (sh_p, sh_s), (sc_p, sc_s), (g_p, g_s) = mod_of(i, 0)
        (sh2_p, sh2_s), (sc2_p, sc2_s), (g2_p, g2_s) = mod_of(i, 1)
        if i % 2 == 0:
            qkv_p = modulated_projection(x_p, sc_p, sh_p, attn_w_qkv[li], cos_p, sin_p, rope=True, tm=1024)
            qkv_s = modulated_projection(x_s, sc_s, sh_s, attn_w_qkv[li], cos_s, sin_s, rope=True, tm=t_s)
            a_p, k_p, v_p = moba_prompt_attention(qkv_p)
            q_s, k_s, v_s = (qkv_s[c].reshape(n_s, dec_seq, d) for c in range(3))
            kmean = paged_block_means(cache_k, page_table, li).reshape(n_s, n_past_blocks, d)
            q_pad = jnp.zeros((n_s, 8, d), F32).at[:, :dec_seq].set(q_s)
            sel = sample_block_select(q_pad, kmean)[:, :, :dec_seq, :MOBA_TOPK]
            sel = sel.transpose(0, 2, 1, 3)
            ppb = MOBA_BLOCK // PAGE_SIZE
            logical = (sel[..., None] * ppb + jnp.arange(ppb)).reshape(n_s, dec_seq, N_HEADS, MOBA_TOPK * ppb)
            phys = page_table[jnp.arange(n_s)[:, None, None, None], logical]
            a_s = moba_sample_attention(q_s, k_s, v_s, cache_k, cache_v, phys, li).reshape(1, t_s, d)
            k_p_rows.append(k_p)
            v_p_rows.append(v_p)
            k_s_rows.append(k_s)
            v_s_rows.append(v_s)
            w_out = attn_w_o[li]
        else:
            bcv_p = modulated_projection(x_p, sc_p, sh_p, conv_w_in[li], cos_p, sin_p, rope=False, tm=1024)
            bcv_s = modulated_projection(x_s, sc_s, sh_s, conv_w_in[li], cos_s, sin_s, rope=False, tm=t_s)
            a_p, st_p = conv_gate(bcv_p, jnp.zeros((n_p, CONV_WIDTH - 1, d), F32), conv_w[li], tl=512)
            a_s, st_s = conv_gate(bcv_s.reshape(3, n_s, dec_seq, d), state_conv[:, li], conv_w[li], tl=dec_seq)
            a_s = a_s.reshape(1, t_s, d)
            conv_p_rows.append(st_p)
            conv_s_rows.append(st_s)
            w_out = conv_w_out[li]
        g1, b1 = ln_g[i, 0][None], ln_b[i, 0][None]
        w_router = jnp.pad(router_w[i], ((0, 0), (0, 128 - N_EXPERTS)))
        x_p, h_p, score_p = out_projection_norm(a_p, w_out, x_p, g_p, g1, b1, sc2_p, sh2_p, w_router, tm=512)
        x_s, h_s, score_s = out_projection_norm(a_s, w_out, x_s, g_s, g1, b1, sc2_s, sh2_s, w_router, tm=t_s)

        h_all = jnp.concatenate([h_p.reshape(t_p, d), h_s.reshape(t_s, d)], axis=0)
        scores = jnp.concatenate([score_p.reshape(t_p, 128), score_s.reshape(t_s, 128),
                                  jnp.zeros((t_pad - t_all, 128), F32)], axis=0)
        idx_t, gate_t, rank_t, counts = route_tokens(scores, router_b[i], t_all)
        dest, row_tok, block_expert, n_used = _dispatch(idx_t, rank_t, counts[:, 0], t_all)
        ys = routed_expert_rows(block_expert, n_used, h_all[row_tok], exp_w_gate[i], exp_w_up[i], exp_w_down[i])
        routed = jnp.sum(ys[dest] * gate_t[:, :t_all, None], axis=0)
        g2, b2 = ln_g[i, 1][None], ln_b[i, 1][None]
        x_p = shared_expert_norm(h_p, x_p, routed[:t_p].reshape(n_p, seq, d), g2_p, g2, b2,
                                 shared_w_gate[i], shared_w_up[i], shared_w_down[i], tm=256)
        x_s = shared_expert_norm(h_s, x_s, routed[t_p:].reshape(1, t_s, d), g2_s, g2, b2,
                                 shared_w_gate[i], shared_w_up[i], shared_w_down[i], tm=t_s)

    heads = lambda rows, n, l: jnp.stack(rows, axis=1).reshape(n, len(rows), l, N_HEADS, HEAD_DIM)
    return (x_p, x_s.reshape(n_s, dec_seq, d),
            jnp.concatenate(k_p_rows, axis=1), jnp.concatenate(v_p_rows, axis=1), jnp.stack(conv_p_rows, axis=1),
            heads(k_s_rows, n_s, dec_seq), heads(v_s_rows, n_s, dec_seq), jnp.stack(conv_s_rows, axis=1))
```

```python
import functools

import jax
import jax.numpy as jnp
from jax import lax
from jax.experimental import pallas as pl
from jax.experimental.pallas import tpu as pltpu

D_MODEL = 2048
DEPTH = 4
PAGE_SIZE = 128
N_HEADS = 16
HEAD_DIM = D_MODEL // N_HEADS
ROPE_THETA = 10000.0
MOBA_BLOCK = 256
MOBA_TOPK = 3
CONV_WIDTH = 3
N_EXPERTS = 64
TOP_K = 8
N_GROUPS = 8
TOPK_GROUPS = 4
D_EXPERT = 512
ROUTED_SCALE = 2.5
DEEPNORM_ALPHA = (2 * DEPTH) ** 0.25
LN_EPS = 1e-5

F32 = jnp.float32
BF16 = jnp.bfloat16
NEG_INF = float("-inf")

VMEM_LIMIT_BYTES = 56 * 1024 * 1024
MOE_ROW_BLOCK = 256
N_COND_ROWS = 16


def _params(n_axes):
    return pltpu.CompilerParams(dimension_semantics=("arbitrary",) * n_axes,
                                vmem_limit_bytes=VMEM_LIMIT_BYTES)


def _silu(x):
    return x / (1.0 + jnp.exp(-x))


def _split_bf16(x):
    hi = x.astype(BF16)
    lo = (x - hi.astype(F32)).astype(BF16)
    return hi, lo


def _dot(a, b):
    return jnp.dot(a, b, preferred_element_type=F32)


def _dot3(a, b):
    a_hi, a_lo = _split_bf16(a)
    b_hi, b_lo = _split_bf16(b)
    return _dot(a_hi, b_hi) + _dot(a_lo, b_hi) + _dot(a_hi, b_lo)


def _ada_kernel(c_ref, w_ref, b_ref, o_ref):
    o_ref[0] = _dot3(_silu(c_ref[...]), w_ref[0]) + b_ref[0]


def ada_modulation_all(c_all, ada_w, ada_b):
    n_sub = ada_w.shape[0]
    tn = 768
    return pl.pallas_call(
        _ada_kernel,
        grid=(n_sub, 3 * D_MODEL // tn),
        in_specs=[
            pl.BlockSpec((N_COND_ROWS, D_MODEL), lambda s, j: (0, 0)),
            pl.BlockSpec((1, D_MODEL, tn), lambda s, j: (s, 0, j)),
            pl.BlockSpec((1, 1, tn), lambda s, j: (s, 0, j)),
        ],
        out_specs=pl.BlockSpec((1, N_COND_ROWS, tn), lambda s, j: (s, 0, j)),
        out_shape=jax.ShapeDtypeStruct((n_sub, N_COND_ROWS, 3 * D_MODEL), F32),
        compiler_params=_params(2),
        name="ada_modulation",
    )(c_all, ada_w, ada_b)


def _proj_kernel(x_ref, sc_ref, sh_ref, w_ref, cos_ref, sin_ref, o_ref, hb_ref, *, rope, tn):
    c = pl.program_id(2)
    j = pl.program_id(3)

    @pl.when((c == 0) & (j == 0))
    def _():
        hb_ref[...] = (x_ref[0] * (1.0 + sc_ref[0]) + sh_ref[0]).astype(BF16)

    acc = _dot(hb_ref[...], w_ref[...].astype(BF16))
    if rope:
        @pl.when(c < 2)
        def _():
            cos = cos_ref[...]
            sin = sin_ref[...]
            for h in range(tn // HEAD_DIM):
                seg = acc[:, h * HEAD_DIM:(h + 1) * HEAD_DIM]
                o_ref[0, 0, :, h * HEAD_DIM:(h + 1) * HEAD_DIM] = (
                    seg * cos + pltpu.roll(seg, HEAD_DIM // 2, axis=1) * sin)

        @pl.when(c == 2)
        def _():
            o_ref[0, 0] = acc
    else:
        o_ref[0, 0] = acc


def modulated_projection(x, sc, sh, w, layer, cos, sin, *, rope, tm):
    nb, seq, _ = x.shape
    r = sc.shape[1]
    tn = 512
    nj = D_MODEL // tn
    return pl.pallas_call(
        functools.partial(_proj_kernel, rope=rope, tn=tn),
        grid=(nb, seq // tm, 3, nj),
        in_specs=[
            pl.BlockSpec((1, tm, D_MODEL), lambda b, m, c, j: (b, m, 0)),
            pl.BlockSpec((1, r, D_MODEL), lambda b, m, c, j: (b, 0, 0)),
            pl.BlockSpec((1, r, D_MODEL), lambda b, m, c, j: (b, 0, 0)),
            pl.BlockSpec((None, D_MODEL, tn), lambda b, m, c, j: (layer, 0, c * nj + j)),
            pl.BlockSpec((tm, HEAD_DIM), lambda b, m, c, j: (m, 0)),
            pl.BlockSpec((tm, HEAD_DIM), lambda b, m, c, j: (m, 0)),
        ],
        out_specs=pl.BlockSpec((1, 1, tm, tn), lambda b, m, c, j: (c, b, m, j)),
        out_shape=jax.ShapeDtypeStruct((3, nb, seq, D_MODEL), F32),
        scratch_shapes=[pltpu.VMEM((tm, D_MODEL), BF16)],
        compiler_params=_params(4),
        name="modulated_projection",
    )(x, sc, sh, w, cos, sin)


KEY_GROUP = 4


HEADS_PER_STEP = 2


def _moba_prompt_kernel(q_ref, k_ref, v_ref, o_ref, knew_ref, vnew_ref, kb_ref, vt_ref, kmean_ref, bias_ref,
                        sem, *, n_blocks):
    b = pl.program_id(0)
    hp = pl.program_id(1)
    qi = pl.program_id(2)
    blk_rows = MOBA_BLOCK
    heads = range(HEADS_PER_STEP)
    lanes = lambda hh: slice(hh * HEAD_DIM, (hh + 1) * HEAD_DIM)

    def cache_copies():
        copies = []
        for hh in heads:
            head = hp * HEADS_PER_STEP + hh
            cols = pl.ds(hh * HEAD_DIM, HEAD_DIM)
            copies.append(pltpu.make_async_copy(k_ref.at[0, 0, :, cols], knew_ref.at[b, 0, :, head, :], sem.at[0, hh]))
            copies.append(pltpu.make_async_copy(v_ref.at[0, 0, :, cols], vnew_ref.at[b, 0, :, head, :], sem.at[1, hh]))
        return copies

    @pl.when(qi == 0)
    def _():
        for c in cache_copies():
            c.start()
        for hh in heads:
            for n in range(n_blocks):
                g, j = divmod(n, KEY_GROUP)
                k_blk = k_ref[0, 0, n * blk_rows:(n + 1) * blk_rows, lanes(hh)]
                kb_ref[hh, g, j * blk_rows:(j + 1) * blk_rows, :] = k_blk.astype(BF16)
                kmean_ref[hh, n:n + 1, :] = jnp.sum(k_blk, axis=0, keepdims=True) * (1.0 / MOBA_BLOCK)
                vt_ref[hh, g, :, j * blk_rows:(j + 1) * blk_rows] = (
                    v_ref[0, 0, n * blk_rows:(n + 1) * blk_rows, lanes(hh)].T.astype(BF16))

    @pl.when(qi == n_blocks - 1)
    def _():
        for c in cache_copies():
            c.wait()

    own_group = qi // KEY_GROUP
    causal = jnp.where(lax.broadcasted_iota(jnp.int32, (blk_rows, blk_rows), 0)
                       <= lax.broadcasted_iota(jnp.int32, (blk_rows, blk_rows), 1), 0.0, NEG_INF)
    qs_t = []
    for hh in heads:
        q_t = q_ref[0, 0, :, lanes(hh)].T
        sc = _dot3(kmean_ref[hh], q_t)
        blk = lax.broadcasted_iota(jnp.int32, sc.shape, 0).astype(F32)
        sc = jnp.where(blk < qi.astype(F32), sc, NEG_INF)
        bias = jnp.full(sc.shape, NEG_INF, F32)
        for _ in range(MOBA_TOPK):
            best = jnp.max(sc, axis=0, keepdims=True)
            first = jnp.min(jnp.where(sc == best, blk, float(n_blocks)), axis=0, keepdims=True)
            pick = (blk == first) & (best > NEG_INF)
            bias = jnp.where(pick, 0.0, bias)
            sc = jnp.where(pick, NEG_INF, sc)
        bias_ref[hh] = bias
        qs_t.append((q_t * (HEAD_DIM ** -0.5)).astype(BF16))

    def scores(hh, g, own):
        s = _dot(kb_ref[hh, g], qs_t[hh])
        pieces = []
        for j in range(KEY_GROUP):
            n = g * KEY_GROUP + j
            mask = bias_ref[hh, pl.ds(n, 1), :]
            if own:
                mask = jnp.where(n == qi, causal, jnp.broadcast_to(mask, causal.shape))
            pieces.append(s[j * blk_rows:(j + 1) * blk_rows] + mask)
        return pieces

    def softmax_terms(hh, g, pieces, m_prev):
        m_new = m_prev
        for piece in pieces:
            col_max = jnp.max(piece, axis=0, keepdims=True)
            m_new = col_max if m_new is None else jnp.maximum(m_new, col_max)
        probs = [jnp.exp(piece - m_new) for piece in pieces]
        l_new = sum(jnp.sum(p, axis=0, keepdims=True) for p in probs)
        pv = _dot(vt_ref[hh, g], jnp.concatenate(probs, axis=0).astype(BF16))
        return m_new, l_new, pv

    carry0 = tuple(softmax_terms(hh, own_group, scores(hh, own_group, own=True), None) for hh in heads)

    def body(g, carry):
        out = []
        for hh in heads:
            m, l, acc = carry[hh]
            m_new, l_new, pv = softmax_terms(hh, g, scores(hh, g, own=False), m)
            alpha = jnp.exp(m - m_new)
            out.append((m_new, alpha * l + l_new, alpha * acc + pv))
        return tuple(out)

    final = lax.fori_loop(0, own_group, body, carry0)
    for hh in heads:
        _, l, acc = final[hh]
        o_ref[0, :, lanes(hh)] = (acc / l).T


def moba_prompt_attention(qkv):
    _, nb, seq, _ = qkv.shape
    n_blocks = seq // MOBA_BLOCK
    assert n_blocks % KEY_GROUP == 0
    n_groups = n_blocks // KEY_GROUP
    cache_shape = jax.ShapeDtypeStruct((nb, 1, seq, N_HEADS, HEAD_DIM), F32)
    hps = HEADS_PER_STEP
    width = hps * HEAD_DIM
    return pl.pallas_call(
        functools.partial(_moba_prompt_kernel, n_blocks=n_blocks),
        grid=(nb, N_HEADS // hps, n_blocks),
        in_specs=[
            pl.BlockSpec((1, 1, MOBA_BLOCK, width), lambda b, h, i: (0, b, i, h)),
            pl.BlockSpec((1, 1, seq, width), lambda b, h, i: (1, b, 0, h)),
            pl.BlockSpec((1, 1, seq, width), lambda b, h, i: (2, b, 0, h)),
        ],
        out_specs=[pl.BlockSpec((1, MOBA_BLOCK, width), lambda b, h, i: (b, i, h)),
                   pl.BlockSpec(memory_space=pl.ANY), pl.BlockSpec(memory_space=pl.ANY)],
        out_shape=[jax.ShapeDtypeStruct((nb, seq, D_MODEL), F32), cache_shape, cache_shape],
        scratch_shapes=[
            pltpu.VMEM((hps, n_groups, KEY_GROUP * MOBA_BLOCK, HEAD_DIM), BF16),
            pltpu.VMEM((hps, n_groups, HEAD_DIM, KEY_GROUP * MOBA_BLOCK), BF16),
            pltpu.VMEM((hps, n_blocks, HEAD_DIM), F32),
            pltpu.VMEM((hps, n_blocks, MOBA_BLOCK), F32),
            pltpu.SemaphoreType.DMA((2, hps)),
        ],
        compiler_params=_params(3),
        name="moba_prompt_attention",
    )(qkv, qkv, qkv)


PAGES_PER_BLOCK = MOBA_BLOCK // PAGE_SIZE
KMEAN_BLOCKS_PER_STEP = 2


def _kmean_kernel(pt_ref, *refs):
    del pt_ref
    page_refs, o_ref = refs[:-1], refs[-1]
    for n in range(KMEAN_BLOCKS_PER_STEP):
        tot = sum(jnp.sum(page_refs[n * PAGES_PER_BLOCK + p][0, 0], axis=0) for p in range(PAGES_PER_BLOCK))
        o_ref[0, n] = tot * (1.0 / MOBA_BLOCK)


def paged_block_means(cache_k, page_table, li):
    nb, n_pages = page_table.shape
    n_blocks = n_pages // PAGES_PER_BLOCK
    pages_per_step = KMEAN_BLOCKS_PER_STEP * PAGES_PER_BLOCK
    assert n_blocks % KMEAN_BLOCKS_PER_STEP == 0
    page_spec = lambda p: pl.BlockSpec(
        (1, 1, PAGE_SIZE, N_HEADS, HEAD_DIM), lambda b, n, pt: (pt[b, n * pages_per_step + p], li, 0, 0, 0))
    return pl.pallas_call(
        _kmean_kernel,
        grid_spec=pltpu.PrefetchScalarGridSpec(
            num_scalar_prefetch=1,
            grid=(nb, n_blocks // KMEAN_BLOCKS_PER_STEP),
            in_specs=[page_spec(p) for p in range(pages_per_step)],
            out_specs=pl.BlockSpec((1, KMEAN_BLOCKS_PER_STEP, N_HEADS, HEAD_DIM), lambda b, n, pt: (b, n, 0, 0)),
        ),
        out_shape=jax.ShapeDtypeStruct((nb, n_blocks, N_HEADS, HEAD_DIM), F32),
        compiler_params=_params(2),
        name="paged_block_means",
    )(page_table, *([cache_k] * pages_per_step))


def _sample_select_kernel(q_ref, km_ref, o_ref, *, n_blocks):
    lane = lax.broadcasted_iota(jnp.int32, (8, 128), 1)
    for h in range(N_HEADS):
        q = q_ref[0, :, h * HEAD_DIM:(h + 1) * HEAD_DIM]
        km = km_ref[0, :, h * HEAD_DIM:(h + 1) * HEAD_DIM]
        q_hi, q_lo = _split_bf16(q)
        k_hi, k_lo = _split_bf16(km)
        nt = (((1,), (1,)), ((), ()))
        sc = (lax.dot_general(q_hi, k_hi, nt, preferred_element_type=F32)
              + lax.dot_general(q_lo, k_hi, nt, preferred_element_type=F32)
              + lax.dot_general(q_hi, k_lo, nt, preferred_element_type=F32))
        blk = lax.broadcasted_iota(jnp.int32, sc.shape, 1).astype(F32)
        out = jnp.zeros((8, 128), F32)
        for t in range(MOBA_TOPK):
            best = jnp.max(sc, axis=1, keepdims=True)
            first = jnp.min(jnp.where(sc == best, blk, float(n_blocks)), axis=1, keepdims=True)
            out = jnp.where(lane == t, first, out)
            sc = jnp.where(blk == first, NEG_INF, sc)
        o_ref[0, h] = out.astype(jnp.int32)


def sample_block_select(q_pad, kmean):
    nb, n_blocks, _ = kmean.shape
    return pl.pallas_call(
        functools.partial(_sample_select_kernel, n_blocks=n_blocks),
        grid=(nb,),
        in_specs=[
            pl.BlockSpec((1, 8, D_MODEL), lambda b: (b, 0, 0)),
            pl.BlockSpec((1, n_blocks, D_MODEL), lambda b: (b, 0, 0)),
        ],
        out_specs=pl.BlockSpec((1, N_HEADS, 8, 128), lambda b: (b, 0, 0, 0)),
        out_shape=jax.ShapeDtypeStruct((nb, N_HEADS, 8, 128), jnp.int32),
        compiler_params=_params(1),
        name="sample_block_select",
    )(q_pad, kmean)


def _sample_attn_kernel(phys_ref, q_ref, ko_ref, vo_ref, ck_ref, cv_ref, o_ref, kbuf, vbuf, sem,
                        *, n_pages, dec_seq, li):
    b = pl.program_id(0)
    h = pl.program_id(1)
    step = b * N_HEADS + h
    slot = step % 2

    def page_copies(step_, slot_):
        b_, h_ = step_ // N_HEADS, step_ % N_HEADS
        copies = []
        for l in range(dec_seq):
            for p in range(n_pages):
                page = phys_ref[((b_ * dec_seq + l) * N_HEADS + h_) * n_pages + p]
                copies.append(pltpu.make_async_copy(ck_ref.at[page, li, :, h_, :], kbuf.at[slot_, l, p], sem.at[0, slot_]))
                copies.append(pltpu.make_async_copy(cv_ref.at[page, li, :, h_, :], vbuf.at[slot_, l, p], sem.at[1, slot_]))
        return copies

    @pl.when(step == 0)
    def _():
        for c in page_copies(step, slot):
            c.start()

    @pl.when(step + 1 < pl.num_programs(0) * N_HEADS)
    def _():
        for c in page_copies(step + 1, 1 - slot):
            c.start()

    for c in page_copies(step, slot):
        c.wait()

    nt = (((1,), (1,)), ((), ()))
    k_own = ko_ref[0]
    row = lax.broadcasted_iota(jnp.int32, (dec_seq, 1), 0)
    for l in range(dec_seq):
        qs = q_ref[0, l:l + 1, :] * (HEAD_DIM ** -0.5)
        q8 = jnp.broadcast_to(qs, (8, HEAD_DIM)).astype(BF16)
        s_sel = [lax.dot_general(q8, kbuf[slot, l, p].astype(BF16), nt, preferred_element_type=F32)[0:1]
                 for p in range(n_pages)]
        s_own = jnp.sum(k_own * qs, axis=1, keepdims=True)
        s_own = jnp.where(row <= l, s_own, NEG_INF)
        m = jnp.max(s_own, axis=0, keepdims=True)
        for s in s_sel:
            m = jnp.maximum(m, jnp.max(s, axis=1, keepdims=True))
        p_own = jnp.exp(s_own - m)
        denom = jnp.sum(p_own, axis=0, keepdims=True)
        acc = jnp.sum(p_own * vo_ref[0], axis=0, keepdims=True)
        for p in range(n_pages):
            pr = jnp.exp(s_sel[p] - m)
            denom = denom + jnp.sum(pr, axis=1, keepdims=True)
            p8 = jnp.broadcast_to(pr, (8, PAGE_SIZE)).astype(BF16)
            acc = acc + _dot(p8, vbuf[slot, l, p].astype(BF16))[0:1]
        o_ref[0, l:l + 1, :] = acc / denom


def moba_sample_attention(q, k_new, v_new, cache_k, cache_v, phys, li):
    nb, dec_seq, _ = q.shape
    n_pages = phys.shape[-1]
    own_spec = pl.BlockSpec((1, dec_seq, HEAD_DIM), lambda b, h, ph: (b, 0, h))
    buf = pltpu.VMEM((2, dec_seq, n_pages, PAGE_SIZE, HEAD_DIM), F32)
    return pl.pallas_call(
        functools.partial(_sample_attn_kernel, n_pages=n_pages, dec_seq=dec_seq, li=li),
        grid_spec=pltpu.PrefetchScalarGridSpec(
            num_scalar_prefetch=1,
            grid=(nb, N_HEADS),
            in_specs=[own_spec, own_spec, own_spec,
                      pl.BlockSpec(memory_space=pl.ANY), pl.BlockSpec(memory_space=pl.ANY)],
            out_specs=own_spec,
            scratch_shapes=[buf, buf, pltpu.SemaphoreType.DMA((2, 2))],
        ),
        out_shape=jax.ShapeDtypeStruct((nb, dec_seq, D_MODEL), F32),
        compiler_params=_params(2),
        name="moba_sample_attention",
    )(phys.reshape(-1), q, k_new, v_new, cache_k, cache_v)


CONV_PAD_ROWS = 8


def _conv_gate_kernel(*refs, tl, has_halo):
    if has_halo:
        b_ref, c_ref, v_ref, ch_ref, vh_ref, hist_ref, w_ref, z_ref, st_ref, ext_ref = refs
    else:
        b_ref, c_ref, v_ref, hist_ref, w_ref, z_ref, st_ref, ext_ref = refs
    l_idx = pl.program_id(1)
    u = c_ref[0, 0] * v_ref[0, 0]
    ext_ref[CONV_PAD_ROWS:CONV_PAD_ROWS + tl, :] = u
    n_hist = CONV_WIDTH - 1

    @pl.when(l_idx == 0)
    def _():
        ext_ref[CONV_PAD_ROWS - n_hist:CONV_PAD_ROWS, :] = hist_ref[0]

    if has_halo:
        @pl.when(l_idx > 0)
        def _():
            ext_ref[0:CONV_PAD_ROWS, :] = ch_ref[0, 0] * vh_ref[0, 0]

    w = w_ref[...]
    conv = u * w[2:3, :]
    conv = conv + ext_ref[CONV_PAD_ROWS - 1:CONV_PAD_ROWS - 1 + tl, :] * w[1:2, :]
    conv = conv + ext_ref[CONV_PAD_ROWS - 2:CONV_PAD_ROWS - 2 + tl, :] * w[0:1, :]
    z_ref[0] = b_ref[0, 0] * conv

    @pl.when(l_idx == pl.num_programs(1) - 1)
    def _():
        st_ref[0] = ext_ref[CONV_PAD_ROWS + tl - n_hist:CONV_PAD_ROWS + tl, :]


def conv_gate(bcv, hist, w_conv, *, tl):
    _, nb, seq, _ = bcv.shape
    n_hist = CONV_WIDTH - 1
    nl = seq // tl
    has_halo = nl > 1
    part = lambda c: pl.BlockSpec((1, 1, tl, D_MODEL), lambda b, l: (c, b, l, 0))
    halo = lambda c: pl.BlockSpec(
        (1, 1, CONV_PAD_ROWS, D_MODEL),
        lambda b, l: (c, b, jnp.maximum(l * (tl // CONV_PAD_ROWS) - 1, 0), 0))
    in_specs = [part(0), part(1), part(2)]
    args = [bcv, bcv, bcv]
    if has_halo:
        in_specs += [halo(1), halo(2)]
        args += [bcv, bcv]
    in_specs += [pl.BlockSpec((1, n_hist, D_MODEL), lambda b, l: (b, 0, 0)),
                 pl.BlockSpec((CONV_WIDTH, D_MODEL), lambda b, l: (0, 0))]
    args += [hist, w_conv]
    return pl.pallas_call(
        functools.partial(_conv_gate_kernel, tl=tl, has_halo=has_halo),
        grid=(nb, nl),
        in_specs=in_specs,
        out_specs=[pl.BlockSpec((1, tl, D_MODEL), lambda b, l: (b, l, 0)),
                   pl.BlockSpec((1, n_hist, D_MODEL), lambda b, l: (b, 0, 0))],
        out_shape=[jax.ShapeDtypeStruct((nb, seq, D_MODEL), F32),
                   jax.ShapeDtypeStruct((nb, n_hist, D_MODEL), F32)],
        scratch_shapes=[pltpu.VMEM((CONV_PAD_ROWS + tl, D_MODEL), F32)],
        compiler_params=_params(2),
        name="conv_gate",
    )(*args)


def _layer_norm(z, g, b):
    mu = jnp.mean(z, axis=-1, keepdims=True)
    d = z - mu
    var = jnp.mean(d * d, axis=-1, keepdims=True)
    return d * lax.rsqrt(var + LN_EPS) * g + b


def _out_norm_kernel(a_ref, w_ref, x_ref, gate_ref, g_ref, b_ref, sc_ref, sh_ref, wr_ref,
                     xo_ref, h_ref, score_ref, acc_ref):
    k = pl.program_id(2)

    @pl.when(k == 0)
    def _():
        acc_ref[...] = jnp.zeros_like(acc_ref)

    acc_ref[...] += _dot(a_ref[0].astype(BF16), w_ref[...].astype(BF16))

    @pl.when(k == pl.num_programs(2) - 1)
    def _():
        xn = _layer_norm(DEEPNORM_ALPHA * x_ref[0] + gate_ref[0] * acc_ref[...], g_ref[...], b_ref[...])
        xo_ref[0] = xn
        h = xn * (1.0 + sc_ref[0]) + sh_ref[0]
        h_ref[0] = h
        logits = _dot3(h, wr_ref[...])
        score_ref[0] = 1.0 / (1.0 + jnp.exp(-logits))


def out_projection_norm(a, w, layer, x, gate, ln_g, ln_b, sc_next, sh_next, w_router, *, tm):
    nb, seq, kdim = a.shape
    r = gate.shape[1]
    tk = 512
    row = pl.BlockSpec((1, tm, D_MODEL), lambda b, m, k: (b, m, 0))
    mod = pl.BlockSpec((1, r, D_MODEL), lambda b, m, k: (b, 0, 0))
    vec = pl.BlockSpec((1, D_MODEL), lambda b, m, k: (0, 0))
    n_lanes = w_router.shape[1]
    return pl.pallas_call(
        _out_norm_kernel,
        grid=(nb, seq // tm, kdim // tk),
        in_specs=[
            pl.BlockSpec((1, tm, tk), lambda b, m, k: (b, m, k)),
            pl.BlockSpec((None, tk, D_MODEL), lambda b, m, k: (layer, k, 0)),
            row, mod, vec, vec, mod, mod,
            pl.BlockSpec((D_MODEL, n_lanes), lambda b, m, k: (0, 0)),
        ],
        out_specs=[row, row, pl.BlockSpec((1, tm, n_lanes), lambda b, m, k: (b, m, 0))],
        out_shape=[jax.ShapeDtypeStruct((nb, seq, D_MODEL), F32),
                   jax.ShapeDtypeStruct((nb, seq, D_MODEL), F32),
                   jax.ShapeDtypeStruct((nb, seq, n_lanes), F32)],
        scratch_shapes=[pltpu.VMEM((tm, D_MODEL), F32)],
        compiler_params=_params(3),
        name="out_projection_norm",
    )(a, w, x, gate, ln_g, ln_b, sc_next, sh_next, w_router)


def _swiglu(x_bf16, wg, wu, wd):
    g = _dot(x_bf16, wg)
    u = _dot(x_bf16, wu)
    return _dot((_silu(g) * u).astype(BF16), wd)


def _experts_kernel(be_ref, nb_ref, xs_ref, wg_ref, wu_ref, wd_ref, ys_ref, wgb, wub, wdb):
    i = pl.program_id(0)
    active = i < nb_ref[0]
    new_expert = (i == 0) | (be_ref[i] != be_ref[jnp.maximum(i - 1, 0)])

    @pl.when(active & new_expert)
    def _():
        wgb[...] = wg_ref[0].astype(BF16)
        wub[...] = wu_ref[0].astype(BF16)
        wdb[...] = wd_ref[0].astype(BF16)

    @pl.when(active)
    def _():
        ys_ref[...] = _swiglu(xs_ref[...].astype(BF16), wgb[...], wub[...], wdb[...])

    @pl.when(jnp.logical_not(active))
    def _():
        ys_ref[...] = jnp.zeros_like(ys_ref)


def routed_expert_rows(block_expert, n_used_blocks, xs, w_gate, w_up, w_down, layer):
    n_rows = xs.shape[0]
    bm = MOE_ROW_BLOCK
    wspec = lambda shape: pl.BlockSpec((None, 1) + shape, lambda i, be, nb: (layer, be[i], 0, 0))
    return pl.pallas_call(
        _experts_kernel,
        grid_spec=pltpu.PrefetchScalarGridSpec(
            num_scalar_prefetch=2,
            grid=(n_rows // bm,),
            in_specs=[
                pl.BlockSpec((bm, D_MODEL), lambda i, be, nb: (i, 0)),
                wspec((D_MODEL, D_EXPERT)), wspec((D_MODEL, D_EXPERT)), wspec((D_EXPERT, D_MODEL)),
            ],
            out_specs=pl.BlockSpec((bm, D_MODEL), lambda i, be, nb: (i, 0)),
            scratch_shapes=[pltpu.VMEM((D_MODEL, D_EXPERT), BF16),
                            pltpu.VMEM((D_MODEL, D_EXPERT), BF16),
                            pltpu.VMEM((D_EXPERT, D_MODEL), BF16)],
        ),
        out_shape=jax.ShapeDtypeStruct((n_rows, D_MODEL), F32),
        compiler_params=_params(1),
        name="routed_experts",
    )(block_expert, n_used_blocks, xs, w_gate, w_up, w_down)


def _shared_norm_kernel(h_ref, x_ref, r_ref, gate_ref, g_ref, b_ref, wg_ref, wu_ref, wd_ref,
                        xo_ref, wgb, wub, wdb):
    @pl.when((pl.program_id(0) == 0) & (pl.program_id(1) == 0))
    def _():
        wgb[...] = wg_ref[...].astype(BF16)
        wub[...] = wu_ref[...].astype(BF16)
        wdb[...] = wd_ref[...].astype(BF16)

    f = r_ref[0] + _swiglu(h_ref[0].astype(BF16), wgb[...], wub[...], wdb[...])
    xo_ref[0] = _layer_norm(DEEPNORM_ALPHA * x_ref[0] + gate_ref[0] * f, g_ref[...], b_ref[...])


def shared_expert_norm(h, x, routed, gate, ln_g, ln_b, ws_gate, ws_up, ws_down, layer, *, tm):
    nb, seq, _ = h.shape
    r = gate.shape[1]
    d_shared = ws_gate.shape[2]
    row = pl.BlockSpec((1, tm, D_MODEL), lambda b, m: (b, m, 0))
    vec = pl.BlockSpec((1, D_MODEL), lambda b, m: (0, 0))
    whole = lambda shape: pl.BlockSpec((None,) + shape, lambda b, m: (layer, 0, 0), pipeline_mode=pl.Buffered(1))
    return pl.pallas_call(
        _shared_norm_kernel,
        grid=(nb, seq // tm),
        in_specs=[row, row, row, pl.BlockSpec((1, r, D_MODEL), lambda b, m: (b, 0, 0)), vec, vec,
                  whole((D_MODEL, d_shared)), whole((D_MODEL, d_shared)), whole((d_shared, D_MODEL))],
        out_specs=row,
        out_shape=jax.ShapeDtypeStruct((nb, seq, D_MODEL), F32),
        scratch_shapes=[pltpu.VMEM((D_MODEL, d_shared), BF16),
                        pltpu.VMEM((D_MODEL, d_shared), BF16),
                        pltpu.VMEM((d_shared, D_MODEL), BF16)],
        compiler_params=_params(2),
        name="shared_expert_norm",
    )(h, x, routed, gate, ln_g, ln_b, ws_gate, ws_up, ws_down)


ROUTE_TILE = 256
GROUP_SIZE = N_EXPERTS // N_GROUPS


def _first_argmax(x, ids, n):
    best = jnp.max(x, axis=0, keepdims=True)
    first = jnp.min(jnp.where(x == best, ids, float(n)), axis=0, keepdims=True)
    return best, first


def _route_kernel(s_ref, b_ref, idx_ref, gate_ref, rank_ref, cnt_ref, carry_ref, *, n_valid):
    i = pl.program_id(0)
    tm = ROUTE_TILE

    @pl.when(i == 0)
    def _():
        carry_ref[...] = jnp.zeros_like(carry_ref)

    raw = s_ref[...].T[:N_EXPERTS]
    biased = raw + b_ref[...]
    member = lax.broadcasted_iota(jnp.int32, (GROUP_SIZE, tm), 0).astype(F32)
    group_id = lax.broadcasted_iota(jnp.int32, (N_GROUPS, tm), 0).astype(F32)
    gs = jnp.zeros((N_GROUPS, tm), F32)
    for g in range(N_GROUPS):
        x = biased[g * GROUP_SIZE:(g + 1) * GROUP_SIZE]
        m1, i1 = _first_argmax(x, member, GROUP_SIZE)
        m2 = jnp.max(jnp.where(member == i1, NEG_INF, x), axis=0, keepdims=True)
        gs = jnp.where(group_id == g, m1 + m2, gs)
    keep = jnp.zeros((N_GROUPS, tm), F32)
    for _ in range(TOPK_GROUPS):
        _, first = _first_argmax(gs, group_id, N_GROUPS)
        pick = group_id == first
        keep = jnp.where(pick, 1.0, keep)
        gs = jnp.where(pick, NEG_INF, gs)
    masked = jnp.concatenate(
        [jnp.where(keep[g:g + 1] > 0.0, biased[g * GROUP_SIZE:(g + 1) * GROUP_SIZE], NEG_INF)
         for g in range(N_GROUPS)], axis=0)
    expert_id = lax.broadcasted_iota(jnp.int32, (N_EXPERTS, tm), 0).astype(F32)
    slot = lax.broadcasted_iota(jnp.int32, (TOP_K, tm), 0)
    idx = jnp.zeros((TOP_K, tm), F32)
    w = jnp.zeros((TOP_K, tm), F32)
    chosen = jnp.zeros((N_EXPERTS, tm), F32)
    picks = []
    for k in range(TOP_K):
        _, first = _first_argmax(masked, expert_id, N_EXPERTS)
        pick = expert_id == first
        picks.append(pick)
        idx = jnp.where(slot == k, first, idx)
        w = jnp.where(slot == k, jnp.sum(jnp.where(pick, raw, 0.0), axis=0, keepdims=True), w)
        chosen = jnp.where(pick, 1.0, chosen)
        masked = jnp.where(pick, NEG_INF, masked)
    gate_ref[...] = w / jnp.sum(w, axis=0, keepdims=True) * ROUTED_SCALE
    idx_ref[...] = idx.astype(jnp.int32)

    tok = lax.broadcasted_iota(jnp.int32, (1, tm), 1) + i * tm
    chosen = jnp.where(tok < n_valid, chosen, 0.0)
    earlier = (lax.broadcasted_iota(jnp.int32, (tm, tm), 0)
               < lax.broadcasted_iota(jnp.int32, (tm, tm), 1)).astype(BF16)
    before = _dot(chosen.astype(BF16), earlier) + carry_ref[...]
    rank = jnp.zeros((TOP_K, tm), F32)
    for k in range(TOP_K):
        rank = jnp.where(slot == k, jnp.sum(jnp.where(picks[k], before, 0.0), axis=0, keepdims=True), rank)
    rank_ref[...] = rank.astype(jnp.int32)
    carry_ref[...] += jnp.sum(chosen, axis=1, keepdims=True)

    @pl.when(i == pl.num_programs(0) - 1)
    def _():
        cnt_ref[...] = jnp.broadcast_to(carry_ref[...], cnt_ref.shape).astype(jnp.int32)


def route_tokens(scores, b_router, n_valid):
    n_pad = scores.shape[0]
    col = pl.BlockSpec((TOP_K, ROUTE_TILE), lambda i: (0, i))
    return pl.pallas_call(
        functools.partial(_route_kernel, n_valid=n_valid),
        grid=(n_pad // ROUTE_TILE,),
        in_specs=[pl.BlockSpec((ROUTE_TILE, scores.shape[1]), lambda i: (i, 0)),
                  pl.BlockSpec((N_EXPERTS, 1), lambda i: (0, 0))],
        out_specs=[col, col, col, pl.BlockSpec((N_EXPERTS, 128), lambda i: (0, 0))],
        out_shape=[jax.ShapeDtypeStruct((TOP_K, n_pad), jnp.int32),
                   jax.ShapeDtypeStruct((TOP_K, n_pad), F32),
                   jax.ShapeDtypeStruct((TOP_K, n_pad), jnp.int32),
                   jax.ShapeDtypeStruct((N_EXPERTS, 128), jnp.int32)],
        scratch_shapes=[pltpu.VMEM((N_EXPERTS, 1), F32)],
        compiler_params=_params(1),
        name="route_tokens",
    )(scores, b_router.reshape(N_EXPERTS, 1))


def _dispatch(idx_t, rank_t, counts, n_tok):
    bm = MOE_ROW_BLOCK
    n_blocks = -(-(n_tok * TOP_K) // bm) + N_EXPERTS
    padded = (counts + bm - 1) // bm * bm
    pad_end = jnp.cumsum(padded)
    pad_start = pad_end - padded
    expert_start = jnp.sum(jnp.where(idx_t[..., None] == jnp.arange(N_EXPERTS), pad_start, 0), axis=-1)
    dest = (expert_start + rank_t)[:, :n_tok]
    tok = jnp.broadcast_to(jnp.arange(n_tok, dtype=jnp.int32)[None], dest.shape)
    row_tok = jnp.zeros((n_blocks * bm,), jnp.int32).at[dest.reshape(-1)].set(tok.reshape(-1))
    block_start = jnp.arange(n_blocks, dtype=jnp.int32) * bm
    block_expert = jnp.minimum(
        jnp.sum((pad_end[None, :] <= block_start[:, None]).astype(jnp.int32), axis=1), N_EXPERTS - 1)
    n_used = (pad_end[-1] // bm).astype(jnp.int32).reshape(1)
    return dest, row_tok, block_expert, n_used


def _rope_tables(pos):
    half = HEAD_DIM // 2
    inv_freq = ROPE_THETA ** (-jnp.arange(half, dtype=F32) / half)
    ang = pos.astype(F32)[:, None] * inv_freq[None, :]
    cos, sin = jnp.cos(ang), jnp.sin(ang)
    return jnp.concatenate([cos, cos], axis=1), jnp.concatenate([-sin, sin], axis=1)


def kernel(x_prompt, x_sample, cache_k, cache_v, state_conv, page_table, c_prompt, c_sample, ada_w, ada_b, ln_g, ln_b, attn_w_qkv, attn_w_o, conv_w_in, conv_w, conv_w_out, router_w, router_b, exp_w_gate, exp_w_up, exp_w_down, shared_w_gate, shared_w_up, shared_w_down):
    n_p, seq, d = x_prompt.shape
    n_s, dec_seq, _ = x_sample.shape
    n_pages = page_table.shape[1]
    past_len = n_pages * PAGE_SIZE
    n_past_blocks = past_len // MOBA_BLOCK
    assert past_len % MOBA_BLOCK == 0 and (past_len + dec_seq - 1) // MOBA_BLOCK == n_past_blocks
    assert n_past_blocks >= MOBA_TOPK and n_p + n_s <= N_COND_ROWS and dec_seq <= 8
    t_p, t_s = n_p * seq, n_s * dec_seq
    t_all = t_p + t_s
    t_pad = -(-t_all // ROUTE_TILE) * ROUTE_TILE

    c_all = jnp.zeros((N_COND_ROWS, d), F32).at[:n_p].set(c_prompt).at[n_p:n_p + n_s].set(c_sample)
    mods = ada_modulation_all(c_all, ada_w.reshape(2 * DEPTH, d, 3 * d), ada_b.reshape(2 * DEPTH, 1, 3 * d))

    def mod_of(i, j):
        m = mods[2 * i + j].reshape(N_COND_ROWS, 3, d)
        m_p = m[:n_p][:, None]
        m_s = jnp.repeat(m[n_p:n_p + n_s], dec_seq, axis=0)[None]
        return [(m_p[:, :, c], m_s[:, :, c]) for c in range(3)]

    cos_p, sin_p = _rope_tables(jnp.arange(seq))
    cos_s, sin_s = _rope_tables(past_len + jnp.tile(jnp.arange(dec_seq), n_s))

    x_p = x_prompt
    x_s = x_sample.reshape(1, t_s, d)
    k_p_rows, v_p_rows, k_s_rows, v_s_rows, conv_p_rows, conv_s_rows = [], [], [], [], [], []
    for i in range(DEPTH):
        li = i // 2
        (sh_p, sh_s), (sc_p, sc_s), (g_p, g_s) = mod_of(i, 0)
        (sh2_p, sh2_s), (sc2_p, sc2_s), (g2_p, g2_s) = mod_of(i, 1)
        if i % 2 == 0:
            qkv_p = modulated_projection(x_p, sc_p, sh_p, attn_w_qkv, li, cos_p, sin_p, rope=True, tm=1024)
            qkv_s = modulated_projection(x_s, sc_s, sh_s, attn_w_qkv, li, cos_s, sin_s, rope=True, tm=t_s)
            a_p, k_p, v_p = moba_prompt_attention(qkv_p)
            q_s, k_s, v_s = (qkv_s[c].reshape(n_s, dec_seq, d) for c in range(3))
            kmean = paged_block_means(cache_k, page_table, li).reshape(n_s, n_past_blocks, d)
            q_pad = jnp.zeros((n_s, 8, d), F32).at[:, :dec_seq].set(q_s)
            sel = sample_block_select(q_pad, kmean)[:, :, :dec_seq, :MOBA_TOPK]
            sel = sel.transpose(0, 2, 1, 3)
            ppb = MOBA_BLOCK // PAGE_SIZE
            logical = (sel[..., None] * ppb + jnp.arange(ppb)).reshape(n_s, dec_seq, N_HEADS, MOBA_TOPK * ppb)
            phys = page_table[jnp.arange(n_s)[:, None, None, None], logical]
            a_s = moba_sample_attention(q_s, k_s, v_s, cache_k, cache_v, phys, li).reshape(1, t_s, d)
            k_p_rows.append(k_p)
            v_p_rows.append(v_p)
            k_s_rows.append(k_s)
            v_s_rows.append(v_s)
            w_out = attn_w_o
        else:
            bcv_p = modulated_projection(x_p, sc_p, sh_p, conv_w_in, li, cos_p, sin_p, rope=False, tm=1024)
            bcv_s = modulated_projection(x_s, sc_s, sh_s, conv_w_in, li, cos_s, sin_s, rope=False, tm=t_s)
            a_p, st_p = conv_gate(bcv_p, jnp.zeros((n_p, CONV_WIDTH - 1, d), F32), conv_w[li], tl=512)
            a_s, st_s = conv_gate(bcv_s.reshape(3, n_s, dec_seq, d), state_conv[:, li], conv_w[li], tl=dec_seq)
            a_s = a_s.reshape(1, t_s, d)
            conv_p_rows.append(st_p)
            conv_s_rows.append(st_s)
            w_out = conv_w_out
        g1, b1 = ln_g[i, 0][None], ln_b[i, 0][None]
        w_router = jnp.pad(router_w[i], ((0, 0), (0, 128 - N_EXPERTS)))
        x_p, h_p, score_p = out_projection_norm(a_p, w_out, li, x_p, g_p, g1, b1, sc2_p, sh2_p, w_router, tm=512)
        x_s, h_s, score_s = out_projection_norm(a_s, w_out, li, x_s, g_s, g1, b1, sc2_s, sh2_s, w_router, tm=t_s)

        h_all = jnp.concatenate([h_p.reshape(t_p, d), h_s.reshape(t_s, d)], axis=0)
        scores = jnp.concatenate([score_p.reshape(t_p, 128), score_s.reshape(t_s, 128),
                                  jnp.zeros((t_pad - t_all, 128), F32)], axis=0)
        idx_t, gate_t, rank_t, counts = route_tokens(scores, router_b[i], t_all)
        dest, row_tok, block_expert, n_used = _dispatch(idx_t, rank_t, counts[:, 0], t_all)
        ys = routed_expert_rows(block_expert, n_used, h_all[row_tok], exp_w_gate, exp_w_up, exp_w_down, i)
        routed = jnp.sum(ys[dest] * gate_t[:, :t_all, None], axis=0)
        g2, b2 = ln_g[i, 1][None], ln_b[i, 1][None]
        x_p = shared_expert_norm(h_p, x_p, routed[:t_p].reshape(n_p, seq, d), g2_p, g2, b2,
                                 shared_w_gate, shared_w_up, shared_w_down, i, tm=256)
        x_s = shared_expert_norm(h_s, x_s, routed[t_p:].reshape(1, t_s, d), g2_s, g2, b2,
                                 shared_w_gate, shared_w_up, shared_w_down, i, tm=t_s)

    heads = lambda rows, n, l: jnp.stack(rows, axis=1).reshape(n, len(rows), l, N_HEADS, HEAD_DIM)
    return (x_p, x_s.reshape(n_s, dec_seq, d),
            jnp.concatenate(k_p_rows, axis=1), jnp.concatenate(v_p_rows, axis=1), jnp.stack(conv_p_rows, axis=1),
            heads(k_s_rows, n_s, dec_seq), heads(v_s_rows, n_s, dec_seq), jnp.stack(conv_s_rows, axis=1))
```

```python
import functools

import jax
import jax.numpy as jnp
from jax import lax
from jax.experimental import pallas as pl
from jax.experimental.pallas import tpu as pltpu

D_MODEL = 2048
DEPTH = 4
PAGE_SIZE = 128
N_HEADS = 16
HEAD_DIM = D_MODEL // N_HEADS
ROPE_THETA = 10000.0
MOBA_BLOCK = 256
MOBA_TOPK = 3
CONV_WIDTH = 3
N_EXPERTS = 64
TOP_K = 8
N_GROUPS = 8
TOPK_GROUPS = 4
D_EXPERT = 512
ROUTED_SCALE = 2.5
DEEPNORM_ALPHA = (2 * DEPTH) ** 0.25
LN_EPS = 1e-5

F32 = jnp.float32
BF16 = jnp.bfloat16
NEG_INF = float("-inf")

VMEM_LIMIT_BYTES = 56 * 1024 * 1024
MOE_ROW_BLOCK = 256
N_COND_ROWS = 16


def _params(n_axes):
    return pltpu.CompilerParams(dimension_semantics=("arbitrary",) * n_axes,
                                vmem_limit_bytes=VMEM_LIMIT_BYTES)


def _silu(x):
    return x / (1.0 + jnp.exp(-x))


def _split_bf16(x):
    hi = x.astype(BF16)
    lo = (x - hi.astype(F32)).astype(BF16)
    return hi, lo


def _dot(a, b):
    return jnp.dot(a, b, preferred_element_type=F32)


def _dot3(a, b):
    a_hi, a_lo = _split_bf16(a)
    b_hi, b_lo = _split_bf16(b)
    return _dot(a_hi, b_hi) + _dot(a_lo, b_hi) + _dot(a_hi, b_lo)


def _ada_kernel(c_ref, w_ref, b_ref, o_ref):
    o_ref[0] = _dot3(_silu(c_ref[...]), w_ref[0]) + b_ref[0]


def ada_modulation_all(c_all, ada_w, ada_b):
    n_sub = ada_w.shape[0]
    tn = 768
    return pl.pallas_call(
        _ada_kernel,
        grid=(n_sub, 3 * D_MODEL // tn),
        in_specs=[
            pl.BlockSpec((N_COND_ROWS, D_MODEL), lambda s, j: (0, 0)),
            pl.BlockSpec((1, D_MODEL, tn), lambda s, j: (s, 0, j)),
            pl.BlockSpec((1, 1, tn), lambda s, j: (s, 0, j)),
        ],
        out_specs=pl.BlockSpec((1, N_COND_ROWS, tn), lambda s, j: (s, 0, j)),
        out_shape=jax.ShapeDtypeStruct((n_sub, N_COND_ROWS, 3 * D_MODEL), F32),
        compiler_params=_params(2),
        name="ada_modulation",
    )(c_all, ada_w, ada_b)


def _proj_kernel(x_ref, sc_ref, sh_ref, w_ref, cos_ref, sin_ref, o_ref, hb_ref, *, rope, tn):
    c = pl.program_id(2)
    j = pl.program_id(3)

    @pl.when((c == 0) & (j == 0))
    def _():
        hb_ref[...] = (x_ref[0] * (1.0 + sc_ref[0]) + sh_ref[0]).astype(BF16)

    acc = _dot(hb_ref[...], w_ref[...].astype(BF16))
    if rope:
        @pl.when(c < 2)
        def _():
            cos = cos_ref[...]
            sin = sin_ref[...]
            for h in range(tn // HEAD_DIM):
                seg = acc[:, h * HEAD_DIM:(h + 1) * HEAD_DIM]
                o_ref[0, 0, :, h * HEAD_DIM:(h + 1) * HEAD_DIM] = (
                    seg * cos + pltpu.roll(seg, HEAD_DIM // 2, axis=1) * sin)

        @pl.when(c == 2)
        def _():
            o_ref[0, 0] = acc
    else:
        o_ref[0, 0] = acc


def modulated_projection(x, sc, sh, w, layer, cos, sin, *, rope, tm):
    nb, seq, _ = x.shape
    r = sc.shape[1]
    tn = 512
    nj = D_MODEL // tn
    return pl.pallas_call(
        functools.partial(_proj_kernel, rope=rope, tn=tn),
        grid=(nb, seq // tm, 3, nj),
        in_specs=[
            pl.BlockSpec((1, tm, D_MODEL), lambda b, m, c, j: (b, m, 0)),
            pl.BlockSpec((1, r, D_MODEL), lambda b, m, c, j: (b, 0, 0)),
            pl.BlockSpec((1, r, D_MODEL), lambda b, m, c, j: (b, 0, 0)),
            pl.BlockSpec((None, D_MODEL, tn), lambda b, m, c, j: (layer, 0, c * nj + j)),
            pl.BlockSpec((tm, HEAD_DIM), lambda b, m, c, j: (m, 0)),
            pl.BlockSpec((tm, HEAD_DIM), lambda b, m, c, j: (m, 0)),
        ],
        out_specs=pl.BlockSpec((1, 1, tm, tn), lambda b, m, c, j: (c, b, m, j)),
        out_shape=jax.ShapeDtypeStruct((3, nb, seq, D_MODEL), F32),
        scratch_shapes=[pltpu.VMEM((tm, D_MODEL), BF16)],
        compiler_params=_params(4),
        name="modulated_projection",
    )(x, sc, sh, w, cos, sin)


KEY_GROUP = 4


HEADS_PER_STEP = 2


def _moba_prompt_kernel(q_ref, k_ref, v_ref, o_ref, knew_ref, vnew_ref, kb_ref, vt_ref, kmean_ref, bias_ref,
                        sem, *, n_blocks):
    b = pl.program_id(0)
    hp = pl.program_id(1)
    qi = pl.program_id(2)
    blk_rows = MOBA_BLOCK
    heads = range(HEADS_PER_STEP)
    lanes = lambda hh: slice(hh * HEAD_DIM, (hh + 1) * HEAD_DIM)

    def cache_copies():
        copies = []
        for hh in heads:
            head = hp * HEADS_PER_STEP + hh
            cols = pl.ds(hh * HEAD_DIM, HEAD_DIM)
            copies.append(pltpu.make_async_copy(k_ref.at[0, 0, :, cols], knew_ref.at[b, 0, :, head, :], sem.at[0, hh]))
            copies.append(pltpu.make_async_copy(v_ref.at[0, 0, :, cols], vnew_ref.at[b, 0, :, head, :], sem.at[1, hh]))
        return copies

    @pl.when(qi == 0)
    def _():
        for c in cache_copies():
            c.start()
        for hh in heads:
            for n in range(n_blocks):
                g, j = divmod(n, KEY_GROUP)
                k_blk = k_ref[0, 0, n * blk_rows:(n + 1) * blk_rows, lanes(hh)]
                kb_ref[hh, g, j * blk_rows:(j + 1) * blk_rows, :] = k_blk.astype(BF16)
                kmean_ref[hh, n:n + 1, :] = jnp.sum(k_blk, axis=0, keepdims=True) * (1.0 / MOBA_BLOCK)
                vt_ref[hh, g, :, j * blk_rows:(j + 1) * blk_rows] = (
                    v_ref[0, 0, n * blk_rows:(n + 1) * blk_rows, lanes(hh)].T.astype(BF16))

    @pl.when(qi == n_blocks - 1)
    def _():
        for c in cache_copies():
            c.wait()

    own_group = qi // KEY_GROUP
    causal = jnp.where(lax.broadcasted_iota(jnp.int32, (blk_rows, blk_rows), 0)
                       <= lax.broadcasted_iota(jnp.int32, (blk_rows, blk_rows), 1), 0.0, NEG_INF)
    qs_t = []
    for hh in heads:
        q_t = q_ref[0, 0, :, lanes(hh)].T
        sc = _dot3(kmean_ref[hh], q_t)
        blk = lax.broadcasted_iota(jnp.int32, sc.shape, 0).astype(F32)
        sc = jnp.where(blk < qi.astype(F32), sc, NEG_INF)
        bias = jnp.full(sc.shape, NEG_INF, F32)
        for _ in range(MOBA_TOPK):
            best = jnp.max(sc, axis=0, keepdims=True)
            first = jnp.min(jnp.where(sc == best, blk, float(n_blocks)), axis=0, keepdims=True)
            pick = (blk == first) & (best > NEG_INF)
            bias = jnp.where(pick, 0.0, bias)
            sc = jnp.where(pick, NEG_INF, sc)
        bias_ref[hh] = bias
        qs_t.append((q_t * (HEAD_DIM ** -0.5)).astype(BF16))

    def scores(hh, g, own):
        s = _dot(kb_ref[hh, g], qs_t[hh])
        pieces = []
        for j in range(KEY_GROUP):
            n = g * KEY_GROUP + j
            mask = bias_ref[hh, pl.ds(n, 1), :]
            if own:
                mask = jnp.where(n == qi, causal, jnp.broadcast_to(mask, causal.shape))
            pieces.append(s[j * blk_rows:(j + 1) * blk_rows] + mask)
        return pieces

    def softmax_terms(hh, g, pieces, m_prev):
        m_new = m_prev
        for piece in pieces:
            col_max = jnp.max(piece, axis=0, keepdims=True)
            m_new = col_max if m_new is None else jnp.maximum(m_new, col_max)
        probs = [jnp.exp(piece - m_new) for piece in pieces]
        l_new = sum(jnp.sum(p, axis=0, keepdims=True) for p in probs)
        pv = _dot(vt_ref[hh, g], jnp.concatenate(probs, axis=0).astype(BF16))
        return m_new, l_new, pv

    carry0 = tuple(softmax_terms(hh, own_group, scores(hh, own_group, own=True), None) for hh in heads)

    def body(g, carry):
        out = []
        for hh in heads:
            m, l, acc = carry[hh]
            m_new, l_new, pv = softmax_terms(hh, g, scores(hh, g, own=False), m)
            alpha = jnp.exp(m - m_new)
            out.append((m_new, alpha * l + l_new, alpha * acc + pv))
        return tuple(out)

    final = lax.fori_loop(0, own_group, body, carry0)
    for hh in heads:
        _, l, acc = final[hh]
        o_ref[0, :, lanes(hh)] = (acc / l).T


def moba_prompt_attention(qkv):
    _, nb, seq, _ = qkv.shape
    n_blocks = seq // MOBA_BLOCK
    assert n_blocks % KEY_GROUP == 0
    n_groups = n_blocks // KEY_GROUP
    cache_shape = jax.ShapeDtypeStruct((nb, 1, seq, N_HEADS, HEAD_DIM), F32)
    hps = HEADS_PER_STEP
    width = hps * HEAD_DIM
    return pl.pallas_call(
        functools.partial(_moba_prompt_kernel, n_blocks=n_blocks),
        grid=(nb, N_HEADS // hps, n_blocks),
        in_specs=[
            pl.BlockSpec((1, 1, MOBA_BLOCK, width), lambda b, h, i: (0, b, i, h)),
            pl.BlockSpec((1, 1, seq, width), lambda b, h, i: (1, b, 0, h)),
            pl.BlockSpec((1, 1, seq, width), lambda b, h, i: (2, b, 0, h)),
        ],
        out_specs=[pl.BlockSpec((1, MOBA_BLOCK, width), lambda b, h, i: (b, i, h)),
                   pl.BlockSpec(memory_space=pl.ANY), pl.BlockSpec(memory_space=pl.ANY)],
        out_shape=[jax.ShapeDtypeStruct((nb, seq, D_MODEL), F32), cache_shape, cache_shape],
        scratch_shapes=[
            pltpu.VMEM((hps, n_groups, KEY_GROUP * MOBA_BLOCK, HEAD_DIM), BF16),
            pltpu.VMEM((hps, n_groups, HEAD_DIM, KEY_GROUP * MOBA_BLOCK), BF16),
            pltpu.VMEM((hps, n_blocks, HEAD_DIM), F32),
            pltpu.VMEM((hps, n_blocks, MOBA_BLOCK), F32),
            pltpu.SemaphoreType.DMA((2, hps)),
        ],
        compiler_params=_params(3),
        name="moba_prompt_attention",
    )(qkv, qkv, qkv)


PAGES_PER_BLOCK = MOBA_BLOCK // PAGE_SIZE
KMEAN_BLOCKS_PER_STEP = 2


def _kmean_kernel(pt_ref, *refs):
    del pt_ref
    page_refs, o_ref = refs[:-1], refs[-1]
    for n in range(KMEAN_BLOCKS_PER_STEP):
        tot = sum(jnp.sum(page_refs[n * PAGES_PER_BLOCK + p][0, 0], axis=0) for p in range(PAGES_PER_BLOCK))
        o_ref[0, n] = tot * (1.0 / MOBA_BLOCK)


def paged_block_means(cache_k, page_table, li):
    nb, n_pages = page_table.shape
    n_blocks = n_pages // PAGES_PER_BLOCK
    pages_per_step = KMEAN_BLOCKS_PER_STEP * PAGES_PER_BLOCK
    assert n_blocks % KMEAN_BLOCKS_PER_STEP == 0
    page_spec = lambda p: pl.BlockSpec(
        (1, 1, PAGE_SIZE, N_HEADS, HEAD_DIM), lambda b, n, pt: (pt[b, n * pages_per_step + p], li, 0, 0, 0))
    return pl.pallas_call(
        _kmean_kernel,
        grid_spec=pltpu.PrefetchScalarGridSpec(
            num_scalar_prefetch=1,
            grid=(nb, n_blocks // KMEAN_BLOCKS_PER_STEP),
            in_specs=[page_spec(p) for p in range(pages_per_step)],
            out_specs=pl.BlockSpec((1, KMEAN_BLOCKS_PER_STEP, N_HEADS, HEAD_DIM), lambda b, n, pt: (b, n, 0, 0)),
        ),
        out_shape=jax.ShapeDtypeStruct((nb, n_blocks, N_HEADS, HEAD_DIM), F32),
        compiler_params=_params(2),
        name="paged_block_means",
    )(page_table, *([cache_k] * pages_per_step))


def _sample_select_kernel(q_ref, km_ref, o_ref, *, n_blocks):
    lane = lax.broadcasted_iota(jnp.int32, (8, 128), 1)
    for h in range(N_HEADS):
        q = q_ref[0, :, h * HEAD_DIM:(h + 1) * HEAD_DIM]
        km = km_ref[0, :, h * HEAD_DIM:(h + 1) * HEAD_DIM]
        q_hi, q_lo = _split_bf16(q)
        k_hi, k_lo = _split_bf16(km)
        nt = (((1,), (1,)), ((), ()))
        sc = (lax.dot_general(q_hi, k_hi, nt, preferred_element_type=F32)
              + lax.dot_general(q_lo, k_hi, nt, preferred_element_type=F32)
              + lax.dot_general(q_hi, k_lo, nt, preferred_element_type=F32))
        blk = lax.broadcasted_iota(jnp.int32, sc.shape, 1).astype(F32)
        out = jnp.zeros((8, 128), F32)
        for t in range(MOBA_TOPK):
            best = jnp.max(sc, axis=1, keepdims=True)
            first = jnp.min(jnp.where(sc == best, blk, float(n_blocks)), axis=1, keepdims=True)
            out = jnp.where(lane == t, first, out)
            sc = jnp.where(blk == first, NEG_INF, sc)
        o_ref[0, h] = out.astype(jnp.int32)


def sample_block_select(q_pad, kmean):
    nb, n_blocks, _ = kmean.shape
    return pl.pallas_call(
        functools.partial(_sample_select_kernel, n_blocks=n_blocks),
        grid=(nb,),
        in_specs=[
            pl.BlockSpec((1, 8, D_MODEL), lambda b: (b, 0, 0)),
            pl.BlockSpec((1, n_blocks, D_MODEL), lambda b: (b, 0, 0)),
        ],
        out_specs=pl.BlockSpec((1, N_HEADS, 8, 128), lambda b: (b, 0, 0, 0)),
        out_shape=jax.ShapeDtypeStruct((nb, N_HEADS, 8, 128), jnp.int32),
        compiler_params=_params(1),
        name="sample_block_select",
    )(q_pad, kmean)


def _sample_attn_kernel(phys_ref, q_ref, ko_ref, vo_ref, ck_ref, cv_ref, o_ref, kbuf, vbuf, sem,
                        *, n_pages, dec_seq, li):
    b = pl.program_id(0)
    h = pl.program_id(1)
    step = b * N_HEADS + h
    slot = step % 2

    def page_copies(step_, slot_):
        b_, h_ = step_ // N_HEADS, step_ % N_HEADS
        copies = []
        for l in range(dec_seq):
            for p in range(n_pages):
                page = phys_ref[((b_ * dec_seq + l) * N_HEADS + h_) * n_pages + p]
                copies.append(pltpu.make_async_copy(ck_ref.at[page, li, :, h_, :], kbuf.at[slot_, l, p], sem.at[0, slot_]))
                copies.append(pltpu.make_async_copy(cv_ref.at[page, li, :, h_, :], vbuf.at[slot_, l, p], sem.at[1, slot_]))
        return copies

    @pl.when(step == 0)
    def _():
        for c in page_copies(step, slot):
            c.start()

    @pl.when(step + 1 < pl.num_programs(0) * N_HEADS)
    def _():
        for c in page_copies(step + 1, 1 - slot):
            c.start()

    for c in page_copies(step, slot):
        c.wait()

    nt = (((1,), (1,)), ((), ()))
    k_own = ko_ref[0]
    row = lax.broadcasted_iota(jnp.int32, (dec_seq, 1), 0)
    for l in range(dec_seq):
        qs = q_ref[0, l:l + 1, :] * (HEAD_DIM ** -0.5)
        q8 = jnp.broadcast_to(qs, (8, HEAD_DIM)).astype(BF16)
        s_sel = [lax.dot_general(q8, kbuf[slot, l, p].astype(BF16), nt, preferred_element_type=F32)[0:1]
                 for p in range(n_pages)]
        s_own = jnp.sum(k_own * qs, axis=1, keepdims=True)
        s_own = jnp.where(row <= l, s_own, NEG_INF)
        m = jnp.max(s_own, axis=0, keepdims=True)
        for s in s_sel:
            m = jnp.maximum(m, jnp.max(s, axis=1, keepdims=True))
        p_own = jnp.exp(s_own - m)
        denom = jnp.sum(p_own, axis=0, keepdims=True)
        acc = jnp.sum(p_own * vo_ref[0], axis=0, keepdims=True)
        for p in range(n_pages):
            pr = jnp.exp(s_sel[p] - m)
            denom = denom + jnp.sum(pr, axis=1, keepdims=True)
            p8 = jnp.broadcast_to(pr, (8, PAGE_SIZE)).astype(BF16)
            acc = acc + _dot(p8, vbuf[slot, l, p].astype(BF16))[0:1]
        o_ref[0, l:l + 1, :] = acc / denom


def moba_sample_attention(q, k_new, v_new, cache_k, cache_v, phys, li):
    nb, dec_seq, _ = q.shape
    n_pages = phys.shape[-1]
    own_spec = pl.BlockSpec((1, dec_seq, HEAD_DIM), lambda b, h, ph: (b, 0, h))
    buf = pltpu.VMEM((2, dec_seq, n_pages, PAGE_SIZE, HEAD_DIM), F32)
    return pl.pallas_call(
        functools.partial(_sample_attn_kernel, n_pages=n_pages, dec_seq=dec_seq, li=li),
        grid_spec=pltpu.PrefetchScalarGridSpec(
            num_scalar_prefetch=1,
            grid=(nb, N_HEADS),
            in_specs=[own_spec, own_spec, own_spec,
                      pl.BlockSpec(memory_space=pl.ANY), pl.BlockSpec(memory_space=pl.ANY)],
            out_specs=own_spec,
            scratch_shapes=[buf, buf, pltpu.SemaphoreType.DMA((2, 2))],
        ),
        out_shape=jax.ShapeDtypeStruct((nb, dec_seq, D_MODEL), F32),
        compiler_params=_params(2),
        name="moba_sample_attention",
    )(phys.reshape(-1), q, k_new, v_new, cache_k, cache_v)


CONV_PAD_ROWS = 8


def _conv_gate_kernel(*refs, tl, has_halo):
    if has_halo:
        b_ref, c_ref, v_ref, ch_ref, vh_ref, hist_ref, w_ref, z_ref, st_ref, ext_ref = refs
    else:
        b_ref, c_ref, v_ref, hist_ref, w_ref, z_ref, st_ref, ext_ref = refs
    l_idx = pl.program_id(1)
    u = c_ref[0, 0] * v_ref[0, 0]
    ext_ref[CONV_PAD_ROWS:CONV_PAD_ROWS + tl, :] = u
    n_hist = CONV_WIDTH - 1

    @pl.when(l_idx == 0)
    def _():
        ext_ref[CONV_PAD_ROWS - n_hist:CONV_PAD_ROWS, :] = hist_ref[0]

    if has_halo:
        @pl.when(l_idx > 0)
        def _():
            ext_ref[0:CONV_PAD_ROWS, :] = ch_ref[0, 0] * vh_ref[0, 0]

    w = w_ref[...]
    conv = u * w[2:3, :]
    conv = conv + ext_ref[CONV_PAD_ROWS - 1:CONV_PAD_ROWS - 1 + tl, :] * w[1:2, :]
    conv = conv + ext_ref[CONV_PAD_ROWS - 2:CONV_PAD_ROWS - 2 + tl, :] * w[0:1, :]
    z_ref[0] = b_ref[0, 0] * conv

    @pl.when(l_idx == pl.num_programs(1) - 1)
    def _():
        st_ref[0] = ext_ref[CONV_PAD_ROWS + tl - n_hist:CONV_PAD_ROWS + tl, :]


def conv_gate(bcv, hist, w_conv, *, tl):
    _, nb, seq, _ = bcv.shape
    n_hist = CONV_WIDTH - 1
    nl = seq // tl
    has_halo = nl > 1
    part = lambda c: pl.BlockSpec((1, 1, tl, D_MODEL), lambda b, l: (c, b, l, 0))
    halo = lambda c: pl.BlockSpec(
        (1, 1, CONV_PAD_ROWS, D_MODEL),
        lambda b, l: (c, b, jnp.maximum(l * (tl // CONV_PAD_ROWS) - 1, 0), 0))
    in_specs = [part(0), part(1), part(2)]
    args = [bcv, bcv, bcv]
    if has_halo:
        in_specs += [halo(1), halo(2)]
        args += [bcv, bcv]
    in_specs += [pl.BlockSpec((1, n_hist, D_MODEL), lambda b, l: (b, 0, 0)),
                 pl.BlockSpec((CONV_WIDTH, D_MODEL), lambda b, l: (0, 0))]
    args += [hist, w_conv]
    return pl.pallas_call(
        functools.partial(_conv_gate_kernel, tl=tl, has_halo=has_halo),
        grid=(nb, nl),
        in_specs=in_specs,
        out_specs=[pl.BlockSpec((1, tl, D_MODEL), lambda b, l: (b, l, 0)),
                   pl.BlockSpec((1, n_hist, D_MODEL), lambda b, l: (b, 0, 0))],
        out_shape=[jax.ShapeDtypeStruct((nb, seq, D_MODEL), F32),
                   jax.ShapeDtypeStruct((nb, n_hist, D_MODEL), F32)],
        scratch_shapes=[pltpu.VMEM((CONV_PAD_ROWS + tl, D_MODEL), F32)],
        compiler_params=_params(2),
        name="conv_gate",
    )(*args)


def _layer_norm(z, g, b):
    mu = jnp.mean(z, axis=-1, keepdims=True)
    d = z - mu
    var = jnp.mean(d * d, axis=-1, keepdims=True)
    return d * lax.rsqrt(var + LN_EPS) * g + b


def _out_norm_kernel(a_ref, w_ref, x_ref, gate_ref, g_ref, b_ref, sc_ref, sh_ref, wr_ref,
                     xo_ref, h_ref, score_ref, acc_ref):
    k = pl.program_id(2)

    @pl.when(k == 0)
    def _():
        acc_ref[...] = jnp.zeros_like(acc_ref)

    acc_ref[...] += _dot(a_ref[0].astype(BF16), w_ref[...].astype(BF16))

    @pl.when(k == pl.num_programs(2) - 1)
    def _():
        xn = _layer_norm(DEEPNORM_ALPHA * x_ref[0] + gate_ref[0] * acc_ref[...], g_ref[...], b_ref[...])
        xo_ref[0] = xn
        h = xn * (1.0 + sc_ref[0]) + sh_ref[0]
        h_ref[0] = h
        logits = _dot3(h, wr_ref[...])
        score_ref[0] = 1.0 / (1.0 + jnp.exp(-logits))


def out_projection_norm(a, w, layer, x, gate, ln_g, ln_b, sc_next, sh_next, w_router, *, tm):
    nb, seq, kdim = a.shape
    r = gate.shape[1]
    tk = 512
    row = pl.BlockSpec((1, tm, D_MODEL), lambda b, m, k: (b, m, 0))
    mod = pl.BlockSpec((1, r, D_MODEL), lambda b, m, k: (b, 0, 0))
    vec = pl.BlockSpec((1, D_MODEL), lambda b, m, k: (0, 0))
    n_lanes = w_router.shape[1]
    return pl.pallas_call(
        _out_norm_kernel,
        grid=(nb, seq // tm, kdim // tk),
        in_specs=[
            pl.BlockSpec((1, tm, tk), lambda b, m, k: (b, m, k)),
            pl.BlockSpec((None, tk, D_MODEL), lambda b, m, k: (layer, k, 0)),
            row, mod, vec, vec, mod, mod,
            pl.BlockSpec((D_MODEL, n_lanes), lambda b, m, k: (0, 0)),
        ],
        out_specs=[row, row, pl.BlockSpec((1, tm, n_lanes), lambda b, m, k: (b, m, 0))],
        out_shape=[jax.ShapeDtypeStruct((nb, seq, D_MODEL), F32),
                   jax.ShapeDtypeStruct((nb, seq, D_MODEL), F32),
                   jax.ShapeDtypeStruct((nb, seq, n_lanes), F32)],
        scratch_shapes=[pltpu.VMEM((tm, D_MODEL), F32)],
        compiler_params=_params(3),
        name="out_projection_norm",
    )(a, w, x, gate, ln_g, ln_b, sc_next, sh_next, w_router)


def _swiglu(x_bf16, wg, wu, wd):
    g = _dot(x_bf16, wg)
    u = _dot(x_bf16, wu)
    return _dot((_silu(g) * u).astype(BF16), wd)


def _experts_kernel(be_ref, nb_ref, xs_ref, wg_ref, wu_ref, wd_ref, ys_ref, wgb, wub, wdb):
    i = pl.program_id(0)
    active = i < nb_ref[0]
    new_expert = (i == 0) | (be_ref[i] != be_ref[jnp.maximum(i - 1, 0)])

    @pl.when(active & new_expert)
    def _():
        wgb[...] = wg_ref[0].astype(BF16)
        wub[...] = wu_ref[0].astype(BF16)
        wdb[...] = wd_ref[0].astype(BF16)

    @pl.when(active)
    def _():
        ys_ref[...] = _swiglu(xs_ref[...].astype(BF16), wgb[...], wub[...], wdb[...])

    @pl.when(jnp.logical_not(active))
    def _():
        ys_ref[...] = jnp.zeros_like(ys_ref)


def routed_expert_rows(block_expert, n_used_blocks, xs, w_gate, w_up, w_down, layer):
    n_rows = xs.shape[0]
    bm = MOE_ROW_BLOCK
    wspec = lambda shape: pl.BlockSpec((None, 1) + shape, lambda i, be, nb: (layer, be[i], 0, 0))
    return pl.pallas_call(
        _experts_kernel,
        grid_spec=pltpu.PrefetchScalarGridSpec(
            num_scalar_prefetch=2,
            grid=(n_rows // bm,),
            in_specs=[
                pl.BlockSpec((bm, D_MODEL), lambda i, be, nb: (i, 0)),
                wspec((D_MODEL, D_EXPERT)), wspec((D_MODEL, D_EXPERT)), wspec((D_EXPERT, D_MODEL)),
            ],
            out_specs=pl.BlockSpec((bm, D_MODEL), lambda i, be, nb: (i, 0)),
            scratch_shapes=[pltpu.VMEM((D_MODEL, D_EXPERT), BF16),
                            pltpu.VMEM((D_MODEL, D_EXPERT), BF16),
                            pltpu.VMEM((D_EXPERT, D_MODEL), BF16)],
        ),
        out_shape=jax.ShapeDtypeStruct((n_rows, D_MODEL), F32),
        compiler_params=_params(1),
        name="routed_experts",
    )(block_expert, n_used_blocks, xs, w_gate, w_up, w_down)


def _shared_norm_kernel(h_ref, x_ref, y_ref, rg_ref, gate_ref, g_ref, b_ref, wg_ref, wu_ref, wd_ref,
                        xo_ref, wgb, wub, wdb):
    @pl.when((pl.program_id(0) == 0) & (pl.program_id(1) == 0))
    def _():
        wgb[...] = wg_ref[...].astype(BF16)
        wub[...] = wu_ref[...].astype(BF16)
        wdb[...] = wd_ref[...].astype(BF16)

    rg = rg_ref[...]
    routed = y_ref[0] * rg[:, 0:1]
    for k in range(1, TOP_K):
        routed = routed + y_ref[k] * rg[:, k:k + 1]
    f = routed + _swiglu(h_ref[0].astype(BF16), wgb[...], wub[...], wdb[...])
    xo_ref[0] = _layer_norm(DEEPNORM_ALPHA * x_ref[0] + gate_ref[0] * f, g_ref[...], b_ref[...])


def shared_expert_norm(h, x, y_tok, route_gate, tok0, gate, ln_g, ln_b, ws_gate, ws_up, ws_down, layer, *, tm):
    nb, seq, _ = h.shape
    r = gate.shape[1]
    d_shared = ws_gate.shape[2]
    assert tok0 % tm == 0 and seq % tm == 0
    tok_block = lambda b, m: tok0 // tm + b * (seq // tm) + m
    row = pl.BlockSpec((1, tm, D_MODEL), lambda b, m: (b, m, 0))
    vec = pl.BlockSpec((1, D_MODEL), lambda b, m: (0, 0))
    whole = lambda shape: pl.BlockSpec((None,) + shape, lambda b, m: (layer, 0, 0), pipeline_mode=pl.Buffered(1))
    return pl.pallas_call(
        _shared_norm_kernel,
        grid=(nb, seq // tm),
        in_specs=[row, row,
                  pl.BlockSpec((TOP_K, tm, D_MODEL), lambda b, m: (0, tok_block(b, m), 0)),
                  pl.BlockSpec((tm, TOP_K), lambda b, m: (tok_block(b, m), 0)),
                  pl.BlockSpec((1, r, D_MODEL), lambda b, m: (b, 0, 0)), vec, vec,
                  whole((D_MODEL, d_shared)), whole((D_MODEL, d_shared)), whole((d_shared, D_MODEL))],
        out_specs=row,
        out_shape=jax.ShapeDtypeStruct((nb, seq, D_MODEL), F32),
        scratch_shapes=[pltpu.VMEM((D_MODEL, d_shared), BF16),
                        pltpu.VMEM((D_MODEL, d_shared), BF16),
                        pltpu.VMEM((d_shared, D_MODEL), BF16)],
        compiler_params=_params(2),
        name="shared_expert_norm",
    )(h, x, y_tok, route_gate, gate, ln_g, ln_b, ws_gate, ws_up, ws_down)


ROUTE_TILE = 256
GROUP_SIZE = N_EXPERTS // N_GROUPS


def _first_argmax(x, ids, n):
    best = jnp.max(x, axis=0, keepdims=True)
    first = jnp.min(jnp.where(x == best, ids, float(n)), axis=0, keepdims=True)
    return best, first


def _route_kernel(s_ref, b_ref, idx_ref, gate_ref, rank_ref, cnt_ref, carry_ref, *, n_valid):
    i = pl.program_id(0)
    tm = ROUTE_TILE

    @pl.when(i == 0)
    def _():
        carry_ref[...] = jnp.zeros_like(carry_ref)

    raw = s_ref[...].T[:N_EXPERTS]
    biased = raw + b_ref[...]
    member = lax.broadcasted_iota(jnp.int32, (GROUP_SIZE, tm), 0).astype(F32)
    group_id = lax.broadcasted_iota(jnp.int32, (N_GROUPS, tm), 0).astype(F32)
    gs = jnp.zeros((N_GROUPS, tm), F32)
    for g in range(N_GROUPS):
        x = biased[g * GROUP_SIZE:(g + 1) * GROUP_SIZE]
        m1, i1 = _first_argmax(x, member, GROUP_SIZE)
        m2 = jnp.max(jnp.where(member == i1, NEG_INF, x), axis=0, keepdims=True)
        gs = jnp.where(group_id == g, m1 + m2, gs)
    keep = jnp.zeros((N_GROUPS, tm), F32)
    for _ in range(TOPK_GROUPS):
        _, first = _first_argmax(gs, group_id, N_GROUPS)
        pick = group_id == first
        keep = jnp.where(pick, 1.0, keep)
        gs = jnp.where(pick, NEG_INF, gs)
    masked = jnp.concatenate(
        [jnp.where(keep[g:g + 1] > 0.0, biased[g * GROUP_SIZE:(g + 1) * GROUP_SIZE], NEG_INF)
         for g in range(N_GROUPS)], axis=0)
    expert_id = lax.broadcasted_iota(jnp.int32, (N_EXPERTS, tm), 0).astype(F32)
    slot = lax.broadcasted_iota(jnp.int32, (TOP_K, tm), 0)
    idx = jnp.zeros((TOP_K, tm), F32)
    w = jnp.zeros((TOP_K, tm), F32)
    chosen = jnp.zeros((N_EXPERTS, tm), F32)
    picks = []
    for k in range(TOP_K):
        _, first = _first_argmax(masked, expert_id, N_EXPERTS)
        pick = expert_id == first
        picks.append(pick)
        idx = jnp.where(slot == k, first, idx)
        w = jnp.where(slot == k, jnp.sum(jnp.where(pick, raw, 0.0), axis=0, keepdims=True), w)
        chosen = jnp.where(pick, 1.0, chosen)
        masked = jnp.where(pick, NEG_INF, masked)
    gate_ref[...] = w / jnp.sum(w, axis=0, keepdims=True) * ROUTED_SCALE
    idx_ref[...] = idx.astype(jnp.int32)

    tok = lax.broadcasted_iota(jnp.int32, (1, tm), 1) + i * tm
    chosen = jnp.where(tok < n_valid, chosen, 0.0)
    earlier = (lax.broadcasted_iota(jnp.int32, (tm, tm), 0)
               < lax.broadcasted_iota(jnp.int32, (tm, tm), 1)).astype(BF16)
    before = _dot(chosen.astype(BF16), earlier) + carry_ref[...]
    rank = jnp.zeros((TOP_K, tm), F32)
    for k in range(TOP_K):
        rank = jnp.where(slot == k, jnp.sum(jnp.where(picks[k], before, 0.0), axis=0, keepdims=True), rank)
    rank_ref[...] = rank.astype(jnp.int32)
    carry_ref[...] += jnp.sum(chosen, axis=1, keepdims=True)

    @pl.when(i == pl.num_programs(0) - 1)
    def _():
        cnt_ref[...] = jnp.broadcast_to(carry_ref[...], cnt_ref.shape).astype(jnp.int32)


def route_tokens(scores, b_router, n_valid):
    n_pad = scores.shape[0]
    col = pl.BlockSpec((TOP_K, ROUTE_TILE), lambda i: (0, i))
    return pl.pallas_call(
        functools.partial(_route_kernel, n_valid=n_valid),
        grid=(n_pad // ROUTE_TILE,),
        in_specs=[pl.BlockSpec((ROUTE_TILE, scores.shape[1]), lambda i: (i, 0)),
                  pl.BlockSpec((N_EXPERTS, 1), lambda i: (0, 0))],
        out_specs=[col, col, col, pl.BlockSpec((N_EXPERTS, 128), lambda i: (0, 0))],
        out_shape=[jax.ShapeDtypeStruct((TOP_K, n_pad), jnp.int32),
                   jax.ShapeDtypeStruct((TOP_K, n_pad), F32),
                   jax.ShapeDtypeStruct((TOP_K, n_pad), jnp.int32),
                   jax.ShapeDtypeStruct((N_EXPERTS, 128), jnp.int32)],
        scratch_shapes=[pltpu.VMEM((N_EXPERTS, 1), F32)],
        compiler_params=_params(1),
        name="route_tokens",
    )(scores, b_router.reshape(N_EXPERTS, 1))


def _dispatch(idx_t, rank_t, counts, n_tok):
    bm = MOE_ROW_BLOCK
    n_blocks = -(-(n_tok * TOP_K) // bm) + N_EXPERTS
    padded = (counts + bm - 1) // bm * bm
    pad_end = jnp.cumsum(padded)
    pad_start = pad_end - padded
    expert_start = jnp.sum(jnp.where(idx_t[..., None] == jnp.arange(N_EXPERTS), pad_start, 0), axis=-1)
    dest = (expert_start + rank_t)[:, :n_tok]
    tok = jnp.broadcast_to(jnp.arange(n_tok, dtype=jnp.int32)[None], dest.shape)
    row_tok = jnp.zeros((n_blocks * bm,), jnp.int32).at[dest.reshape(-1)].set(tok.reshape(-1))
    block_start = jnp.arange(n_blocks, dtype=jnp.int32) * bm
    block_expert = jnp.minimum(
        jnp.sum((pad_end[None, :] <= block_start[:, None]).astype(jnp.int32), axis=1), N_EXPERTS - 1)
    n_used = (pad_end[-1] // bm).astype(jnp.int32).reshape(1)
    return dest, row_tok, block_expert, n_used


def _rope_tables(pos):
    half = HEAD_DIM // 2
    inv_freq = ROPE_THETA ** (-jnp.arange(half, dtype=F32) / half)
    ang = pos.astype(F32)[:, None] * inv_freq[None, :]
    cos, sin = jnp.cos(ang), jnp.sin(ang)
    return jnp.concatenate([cos, cos], axis=1), jnp.concatenate([-sin, sin], axis=1)


def kernel(x_prompt, x_sample, cache_k, cache_v, state_conv, page_table, c_prompt, c_sample, ada_w, ada_b, ln_g, ln_b, attn_w_qkv, attn_w_o, conv_w_in, conv_w, conv_w_out, router_w, router_b, exp_w_gate, exp_w_up, exp_w_down, shared_w_gate, shared_w_up, shared_w_down):
    n_p, seq, d = x_prompt.shape
    n_s, dec_seq, _ = x_sample.shape
    n_pages = page_table.shape[1]
    past_len = n_pages * PAGE_SIZE
    n_past_blocks = past_len // MOBA_BLOCK
    assert past_len % MOBA_BLOCK == 0 and (past_len + dec_seq - 1) // MOBA_BLOCK == n_past_blocks
    assert n_past_blocks >= MOBA_TOPK and n_p + n_s <= N_COND_ROWS and dec_seq <= 8
    t_p, t_s = n_p * seq, n_s * dec_seq
    t_all = t_p + t_s
    t_pad = -(-t_all // ROUTE_TILE) * ROUTE_TILE

    c_all = jnp.zeros((N_COND_ROWS, d), F32).at[:n_p].set(c_prompt).at[n_p:n_p + n_s].set(c_sample)
    mods = ada_modulation_all(c_all, ada_w.reshape(2 * DEPTH, d, 3 * d), ada_b.reshape(2 * DEPTH, 1, 3 * d))

    def mod_of(i, j):
        m = mods[2 * i + j].reshape(N_COND_ROWS, 3, d)
        m_p = m[:n_p][:, None]
        m_s = jnp.repeat(m[n_p:n_p + n_s], dec_seq, axis=0)[None]
        return [(m_p[:, :, c], m_s[:, :, c]) for c in range(3)]

    cos_p, sin_p = _rope_tables(jnp.arange(seq))
    cos_s, sin_s = _rope_tables(past_len + jnp.tile(jnp.arange(dec_seq), n_s))

    x_p = x_prompt
    x_s = x_sample.reshape(1, t_s, d)
    k_p_rows, v_p_rows, k_s_rows, v_s_rows, conv_p_rows, conv_s_rows = [], [], [], [], [], []
    for i in range(DEPTH):
        li = i // 2
        (sh_p, sh_s), (sc_p, sc_s), (g_p, g_s) = mod_of(i, 0)
        (sh2_p, sh2_s), (sc2_p, sc2_s), (g2_p, g2_s) = mod_of(i, 1)
        if i % 2 == 0:
            qkv_p = modulated_projection(x_p, sc_p, sh_p, attn_w_qkv, li, cos_p, sin_p, rope=True, tm=1024)
            qkv_s = modulated_projection(x_s, sc_s, sh_s, attn_w_qkv, li, cos_s, sin_s, rope=True, tm=t_s)
            a_p, k_p, v_p = moba_prompt_attention(qkv_p)
            q_s, k_s, v_s = (qkv_s[c].reshape(n_s, dec_seq, d) for c in range(3))
            kmean = paged_block_means(cache_k, page_table, li).reshape(n_s, n_past_blocks, d)
            q_pad = jnp.zeros((n_s, 8, d), F32).at[:, :dec_seq].set(q_s)
            sel = sample_block_select(q_pad, kmean)[:, :, :dec_seq, :MOBA_TOPK]
            sel = sel.transpose(0, 2, 1, 3)
            ppb = MOBA_BLOCK // PAGE_SIZE
            logical = (sel[..., None] * ppb + jnp.arange(ppb)).reshape(n_s, dec_seq, N_HEADS, MOBA_TOPK * ppb)
            phys = page_table[jnp.arange(n_s)[:, None, None, None], logical]
            a_s = moba_sample_attention(q_s, k_s, v_s, cache_k, cache_v, phys, li).reshape(1, t_s, d)
            k_p_rows.append(k_p)
            v_p_rows.append(v_p)
            k_s_rows.append(k_s)
            v_s_rows.append(v_s)
            w_out = attn_w_o
        else:
            bcv_p = modulated_projection(x_p, sc_p, sh_p, conv_w_in, li, cos_p, sin_p, rope=False, tm=1024)
            bcv_s = modulated_projection(x_s, sc_s, sh_s, conv_w_in, li, cos_s, sin_s, rope=False, tm=t_s)
            a_p, st_p = conv_gate(bcv_p, jnp.zeros((n_p, CONV_WIDTH - 1, d), F32), conv_w[li], tl=512)
            a_s, st_s = conv_gate(bcv_s.reshape(3, n_s, dec_seq, d), state_conv[:, li], conv_w[li], tl=dec_seq)
            a_s = a_s.reshape(1, t_s, d)
            conv_p_rows.append(st_p)
            conv_s_rows.append(st_s)
            w_out = conv_w_out
        g1, b1 = ln_g[i, 0][None], ln_b[i, 0][None]
        w_router = jnp.pad(router_w[i], ((0, 0), (0, 128 - N_EXPERTS)))
        x_p, h_p, score_p = out_projection_norm(a_p, w_out, li, x_p, g_p, g1, b1, sc2_p, sh2_p, w_router, tm=512)
        x_s, h_s, score_s = out_projection_norm(a_s, w_out, li, x_s, g_s, g1, b1, sc2_s, sh2_s, w_router, tm=t_s)

        h_all = jnp.concatenate([h_p.reshape(t_p, d), h_s.reshape(t_s, d)], axis=0)
        scores = jnp.concatenate([score_p.reshape(t_p, 128), score_s.reshape(t_s, 128),
                                  jnp.zeros((t_pad - t_all, 128), F32)], axis=0)
        idx_t, gate_t, rank_t, counts = route_tokens(scores, router_b[i], t_all)
        dest, row_tok, block_expert, n_used = _dispatch(idx_t, rank_t, counts[:, 0], t_all)
        ys = routed_expert_rows(block_expert, n_used, h_all[row_tok], exp_w_gate, exp_w_up, exp_w_down, i)
        y_tok = ys[dest]
        route_gate = gate_t[:, :t_all].T
        g2, b2 = ln_g[i, 1][None], ln_b[i, 1][None]
        x_p = shared_expert_norm(h_p, x_p, y_tok, route_gate, 0, g2_p, g2, b2,
                                 shared_w_gate, shared_w_up, shared_w_down, i, tm=128)
        x_s = shared_expert_norm(h_s, x_s, y_tok, route_gate, t_p, g2_s, g2, b2,
                                 shared_w_gate, shared_w_up, shared_w_down, i, tm=t_s)

    heads = lambda rows, n, l: jnp.stack(rows, axis=1).reshape(n, len(rows), l, N_HEADS, HEAD_DIM)
    return (x_p, x_s.reshape(n_s, dec_seq, d),
            jnp.concatenate(k_p_rows, axis=1), jnp.concatenate(v_p_rows, axis=1), jnp.stack(conv_p_rows, axis=1),
            heads(k_s_rows, n_s, dec_seq), heads(v_s_rows, n_s, dec_seq), jnp.stack(conv_s_rows, axis=1))
```

```python
import functools

import jax
import jax.numpy as jnp
from jax import lax
from jax.experimental import pallas as pl
from jax.experimental.pallas import tpu as pltpu

D_MODEL = 2048
DEPTH = 4
PAGE_SIZE = 128
N_HEADS = 16
HEAD_DIM = D_MODEL // N_HEADS
ROPE_THETA = 10000.0
MOBA_BLOCK = 256
MOBA_TOPK = 3
CONV_WIDTH = 3
N_EXPERTS = 64
TOP_K = 8
N_GROUPS = 8
TOPK_GROUPS = 4
D_EXPERT = 512
ROUTED_SCALE = 2.5
DEEPNORM_ALPHA = (2 * DEPTH) ** 0.25
LN_EPS = 1e-5

F32 = jnp.float32
BF16 = jnp.bfloat16
NEG_INF = float("-inf")

VMEM_LIMIT_BYTES = 56 * 1024 * 1024
MOE_ROW_BLOCK = 256
N_COND_ROWS = 16


def _params(n_axes):
    return pltpu.CompilerParams(dimension_semantics=("arbitrary",) * n_axes,
                                vmem_limit_bytes=VMEM_LIMIT_BYTES)


def _silu(x):
    return x / (1.0 + jnp.exp(-x))


def _split_bf16(x):
    hi = x.astype(BF16)
    lo = (x - hi.astype(F32)).astype(BF16)
    return hi, lo


def _dot(a, b):
    return jnp.dot(a, b, preferred_element_type=F32)


def _dot3(a, b):
    a_hi, a_lo = _split_bf16(a)
    b_hi, b_lo = _split_bf16(b)
    return _dot(a_hi, b_hi) + _dot(a_lo, b_hi) + _dot(a_hi, b_lo)


def _ada_kernel(c_ref, w_ref, b_ref, o_ref):
    o_ref[0] = _dot3(_silu(c_ref[...]), w_ref[0]) + b_ref[0]


def ada_modulation_all(c_all, ada_w, ada_b):
    n_sub = ada_w.shape[0]
    tn = 768
    return pl.pallas_call(
        _ada_kernel,
        grid=(n_sub, 3 * D_MODEL // tn),
        in_specs=[
            pl.BlockSpec((N_COND_ROWS, D_MODEL), lambda s, j: (0, 0)),
            pl.BlockSpec((1, D_MODEL, tn), lambda s, j: (s, 0, j)),
            pl.BlockSpec((1, 1, tn), lambda s, j: (s, 0, j)),
        ],
        out_specs=pl.BlockSpec((1, N_COND_ROWS, tn), lambda s, j: (s, 0, j)),
        out_shape=jax.ShapeDtypeStruct((n_sub, N_COND_ROWS, 3 * D_MODEL), F32),
        compiler_params=_params(2),
        name="ada_modulation",
    )(c_all, ada_w, ada_b)


def _proj_kernel(x_ref, sc_ref, sh_ref, w_ref, cos_ref, sin_ref, o_ref, hb_ref, *, rope, tn):
    c = pl.program_id(2)
    j = pl.program_id(3)

    @pl.when((c == 0) & (j == 0))
    def _():
        hb_ref[...] = (x_ref[0] * (1.0 + sc_ref[0]) + sh_ref[0]).astype(BF16)

    acc = _dot(hb_ref[...], w_ref[...].astype(BF16))
    if rope:
        @pl.when(c < 2)
        def _():
            cos = cos_ref[...]
            sin = sin_ref[...]
            for h in range(tn // HEAD_DIM):
                seg = acc[:, h * HEAD_DIM:(h + 1) * HEAD_DIM]
                o_ref[0, 0, :, h * HEAD_DIM:(h + 1) * HEAD_DIM] = (
                    seg * cos + pltpu.roll(seg, HEAD_DIM // 2, axis=1) * sin)

        @pl.when(c == 2)
        def _():
            o_ref[0, 0] = acc
    else:
        o_ref[0, 0] = acc


def modulated_projection(x, sc, sh, w, layer, cos, sin, *, rope, tm):
    nb, seq, _ = x.shape
    r = sc.shape[1]
    tn = 512
    nj = D_MODEL // tn
    return pl.pallas_call(
        functools.partial(_proj_kernel, rope=rope, tn=tn),
        grid=(nb, seq // tm, 3, nj),
        in_specs=[
            pl.BlockSpec((1, tm, D_MODEL), lambda b, m, c, j: (b, m, 0)),
            pl.BlockSpec((1, r, D_MODEL), lambda b, m, c, j: (b, 0, 0)),
            pl.BlockSpec((1, r, D_MODEL), lambda b, m, c, j: (b, 0, 0)),
            pl.BlockSpec((None, D_MODEL, tn), lambda b, m, c, j: (layer, 0, c * nj + j)),
            pl.BlockSpec((tm, HEAD_DIM), lambda b, m, c, j: (m, 0)),
            pl.BlockSpec((tm, HEAD_DIM), lambda b, m, c, j: (m, 0)),
        ],
        out_specs=pl.BlockSpec((1, 1, tm, tn), lambda b, m, c, j: (c, b, m, j)),
        out_shape=jax.ShapeDtypeStruct((3, nb, seq, D_MODEL), F32),
        scratch_shapes=[pltpu.VMEM((tm, D_MODEL), BF16)],
        compiler_params=_params(4),
        name="modulated_projection",
    )(x, sc, sh, w, cos, sin)


KEY_GROUP = 4


HEADS_PER_STEP = 2


def _moba_prompt_kernel(q_ref, k_ref, v_ref, o_ref, knew_ref, vnew_ref, kb_ref, vt_ref, kmean_ref, bias_ref,
                        m_ref, l_ref, acc_ref, sem, *, n_blocks):
    b = pl.program_id(0)
    hp = pl.program_id(1)
    qi = pl.program_id(2)
    blk_rows = MOBA_BLOCK
    heads = range(HEADS_PER_STEP)
    lanes = lambda hh: slice(hh * HEAD_DIM, (hh + 1) * HEAD_DIM)

    def cache_copies():
        copies = []
        for hh in heads:
            head = hp * HEADS_PER_STEP + hh
            cols = pl.ds(hh * HEAD_DIM, HEAD_DIM)
            copies.append(pltpu.make_async_copy(k_ref.at[0, 0, :, cols], knew_ref.at[b, 0, :, head, :], sem.at[0, hh]))
            copies.append(pltpu.make_async_copy(v_ref.at[0, 0, :, cols], vnew_ref.at[b, 0, :, head, :], sem.at[1, hh]))
        return copies

    @pl.when(qi == 0)
    def _():
        for c in cache_copies():
            c.start()
        for hh in heads:
            for n in range(n_blocks):
                g, j = divmod(n, KEY_GROUP)
                k_blk = k_ref[0, 0, n * blk_rows:(n + 1) * blk_rows, lanes(hh)]
                kb_ref[hh, g, j * blk_rows:(j + 1) * blk_rows, :] = k_blk.astype(BF16)
                kmean_ref[hh, n:n + 1, :] = jnp.sum(k_blk, axis=0, keepdims=True) * (1.0 / MOBA_BLOCK)
                vt_ref[hh, g, :, j * blk_rows:(j + 1) * blk_rows] = (
                    v_ref[0, 0, n * blk_rows:(n + 1) * blk_rows, lanes(hh)].T.astype(BF16))

    @pl.when(qi == n_blocks - 1)
    def _():
        for c in cache_copies():
            c.wait()

    own_group = qi // KEY_GROUP
    causal = jnp.where(lax.broadcasted_iota(jnp.int32, (blk_rows, blk_rows), 0)
                       <= lax.broadcasted_iota(jnp.int32, (blk_rows, blk_rows), 1), 0.0, NEG_INF)
    qs_t = []
    for hh in heads:
        q_t = q_ref[0, 0, :, lanes(hh)].T
        sc = _dot3(kmean_ref[hh], q_t)
        blk = lax.broadcasted_iota(jnp.int32, sc.shape, 0).astype(F32)
        sc = jnp.where(blk < qi.astype(F32), sc, NEG_INF)
        bias = jnp.full(sc.shape, NEG_INF, F32)
        for _ in range(MOBA_TOPK):
            best = jnp.max(sc, axis=0, keepdims=True)
            first = jnp.min(jnp.where(sc == best, blk, float(n_blocks)), axis=0, keepdims=True)
            pick = (blk == first) & (best > NEG_INF)
            bias = jnp.where(pick, 0.0, bias)
            sc = jnp.where(pick, NEG_INF, sc)
        bias_ref[hh] = bias
        qs_t.append((q_t * (HEAD_DIM ** -0.5)).astype(BF16))

    def scores(hh, g, n_own=None):
        n_pieces = KEY_GROUP if n_own is None else n_own + 1
        s = _dot(kb_ref[hh, g, 0:n_pieces * blk_rows, :], qs_t[hh])
        pieces = []
        for j in range(n_pieces):
            mask = causal if j == n_own else bias_ref[hh, pl.ds(g * KEY_GROUP + j, 1), :]
            pieces.append(s[j * blk_rows:(j + 1) * blk_rows] + mask)
        return pieces

    def softmax_terms(hh, g, pieces, m_prev):
        m_new = m_prev
        for piece in pieces:
            col_max = jnp.max(piece, axis=0, keepdims=True)
            m_new = col_max if m_new is None else jnp.maximum(m_new, col_max)
        probs = [jnp.exp(piece - m_new) for piece in pieces]
        l_new = sum(jnp.sum(p, axis=0, keepdims=True) for p in probs)
        values_t = vt_ref[hh, g, :, 0:len(pieces) * blk_rows]
        pv = _dot(values_t, jnp.concatenate(probs, axis=0).astype(BF16))
        return m_new, l_new, pv

    for n_own in range(KEY_GROUP):
        @pl.when(qi % KEY_GROUP == n_own)
        def _():
            for hh in heads:
                m_ref[hh], l_ref[hh], acc_ref[hh] = softmax_terms(hh, own_group, scores(hh, own_group, n_own), None)

    carry0 = tuple((m_ref[hh], l_ref[hh], acc_ref[hh]) for hh in heads)

    def body(g, carry):
        out = []
        for hh in heads:
            m, l, acc = carry[hh]
            m_new, l_new, pv = softmax_terms(hh, g, scores(hh, g), m)
            alpha = jnp.exp(m - m_new)
            out.append((m_new, alpha * l + l_new, alpha * acc + pv))
        return tuple(out)

    final = lax.fori_loop(0, own_group, body, carry0)
    for hh in heads:
        _, l, acc = final[hh]
        o_ref[0, :, lanes(hh)] = (acc / l).T


def moba_prompt_attention(qkv):
    _, nb, seq, _ = qkv.shape
    n_blocks = seq // MOBA_BLOCK
    assert n_blocks % KEY_GROUP == 0
    n_groups = n_blocks // KEY_GROUP
    cache_shape = jax.ShapeDtypeStruct((nb, 1, seq, N_HEADS, HEAD_DIM), F32)
    hps = HEADS_PER_STEP
    width = hps * HEAD_DIM
    return pl.pallas_call(
        functools.partial(_moba_prompt_kernel, n_blocks=n_blocks),
        grid=(nb, N_HEADS // hps, n_blocks),
        in_specs=[
            pl.BlockSpec((1, 1, MOBA_BLOCK, width), lambda b, h, i: (0, b, i, h)),
            pl.BlockSpec((1, 1, seq, width), lambda b, h, i: (1, b, 0, h)),
            pl.BlockSpec((1, 1, seq, width), lambda b, h, i: (2, b, 0, h)),
        ],
        out_specs=[pl.BlockSpec((1, MOBA_BLOCK, width), lambda b, h, i: (b, i, h)),
                   pl.BlockSpec(memory_space=pl.ANY), pl.BlockSpec(memory_space=pl.ANY)],
        out_shape=[jax.ShapeDtypeStruct((nb, seq, D_MODEL), F32), cache_shape, cache_shape],
        scratch_shapes=[
            pltpu.VMEM((hps, n_groups, KEY_GROUP * MOBA_BLOCK, HEAD_DIM), BF16),
            pltpu.VMEM((hps, n_groups, HEAD_DIM, KEY_GROUP * MOBA_BLOCK), BF16),
            pltpu.VMEM((hps, n_blocks, HEAD_DIM), F32),
            pltpu.VMEM((hps, n_blocks, MOBA_BLOCK), F32),
            pltpu.VMEM((hps, 1, MOBA_BLOCK), F32),
            pltpu.VMEM((hps, 1, MOBA_BLOCK), F32),
            pltpu.VMEM((hps, HEAD_DIM, MOBA_BLOCK), F32),
            pltpu.SemaphoreType.DMA((2, hps)),
        ],
        compiler_params=_params(3),
        name="moba_prompt_attention",
    )(qkv, qkv, qkv)


PAGES_PER_BLOCK = MOBA_BLOCK // PAGE_SIZE
KMEAN_BLOCKS_PER_STEP = 2


def _kmean_kernel(pt_ref, *refs):
    del pt_ref
    page_refs, o_ref = refs[:-1], refs[-1]
    for n in range(KMEAN_BLOCKS_PER_STEP):
        tot = sum(jnp.sum(page_refs[n * PAGES_PER_BLOCK + p][0, 0], axis=0) for p in range(PAGES_PER_BLOCK))
        o_ref[0, n] = tot * (1.0 / MOBA_BLOCK)


def paged_block_means(cache_k, page_table, li):
    nb, n_pages = page_table.shape
    n_blocks = n_pages // PAGES_PER_BLOCK
    pages_per_step = KMEAN_BLOCKS_PER_STEP * PAGES_PER_BLOCK
    assert n_blocks % KMEAN_BLOCKS_PER_STEP == 0
    page_spec = lambda p: pl.BlockSpec(
        (1, 1, PAGE_SIZE, N_HEADS, HEAD_DIM), lambda b, n, pt: (pt[b, n * pages_per_step + p], li, 0, 0, 0))
    return pl.pallas_call(
        _kmean_kernel,
        grid_spec=pltpu.PrefetchScalarGridSpec(
            num_scalar_prefetch=1,
            grid=(nb, n_blocks // KMEAN_BLOCKS_PER_STEP),
            in_specs=[page_spec(p) for p in range(pages_per_step)],
            out_specs=pl.BlockSpec((1, KMEAN_BLOCKS_PER_STEP, N_HEADS, HEAD_DIM), lambda b, n, pt: (b, n, 0, 0)),
        ),
        out_shape=jax.ShapeDtypeStruct((nb, n_blocks, N_HEADS, HEAD_DIM), F32),
        compiler_params=_params(2),
        name="paged_block_means",
    )(page_table, *([cache_k] * pages_per_step))


def _sample_select_kernel(q_ref, km_ref, o_ref, *, n_blocks):
    lane = lax.broadcasted_iota(jnp.int32, (8, 128), 1)
    for h in range(N_HEADS):
        q = q_ref[0, :, h * HEAD_DIM:(h + 1) * HEAD_DIM]
        km = km_ref[0, :, h * HEAD_DIM:(h + 1) * HEAD_DIM]
        q_hi, q_lo = _split_bf16(q)
        k_hi, k_lo = _split_bf16(km)
        nt = (((1,), (1,)), ((), ()))
        sc = (lax.dot_general(q_hi, k_hi, nt, preferred_element_type=F32)
              + lax.dot_general(q_lo, k_hi, nt, preferred_element_type=F32)
              + lax.dot_general(q_hi, k_lo, nt, preferred_element_type=F32))
        blk = lax.broadcasted_iota(jnp.int32, sc.shape, 1).astype(F32)
        out = jnp.zeros((8, 128), F32)
        for t in range(MOBA_TOPK):
            best = jnp.max(sc, axis=1, keepdims=True)
            first = jnp.min(jnp.where(sc == best, blk, float(n_blocks)), axis=1, keepdims=True)
            out = jnp.where(lane == t, first, out)
            sc = jnp.where(blk == first, NEG_INF, sc)
        o_ref[0, h] = out.astype(jnp.int32)


def sample_block_select(q_pad, kmean):
    nb, n_blocks, _ = kmean.shape
    return pl.pallas_call(
        functools.partial(_sample_select_kernel, n_blocks=n_blocks),
        grid=(nb,),
        in_specs=[
            pl.BlockSpec((1, 8, D_MODEL), lambda b: (b, 0, 0)),
            pl.BlockSpec((1, n_blocks, D_MODEL), lambda b: (b, 0, 0)),
        ],
        out_specs=pl.BlockSpec((1, N_HEADS, 8, 128), lambda b: (b, 0, 0, 0)),
        out_shape=jax.ShapeDtypeStruct((nb, N_HEADS, 8, 128), jnp.int32),
        compiler_params=_params(1),
        name="sample_block_select",
    )(q_pad, kmean)


def _sample_attn_kernel(phys_ref, q_ref, ko_ref, vo_ref, ck_ref, cv_ref, o_ref, kbuf, vbuf, sem,
                        *, n_pages, dec_seq, li):
    b = pl.program_id(0)
    h = pl.program_id(1)
    step = b * N_HEADS + h
    slot = step % 2

    def page_copies(step_, slot_):
        b_, h_ = step_ // N_HEADS, step_ % N_HEADS
        copies = []
        for l in range(dec_seq):
            for p in range(n_pages):
                page = phys_ref[((b_ * dec_seq + l) * N_HEADS + h_) * n_pages + p]
                copies.append(pltpu.make_async_copy(ck_ref.at[page, li, :, h_, :], kbuf.at[slot_, l, p], sem.at[0, slot_]))
                copies.append(pltpu.make_async_copy(cv_ref.at[page, li, :, h_, :], vbuf.at[slot_, l, p], sem.at[1, slot_]))
        return copies

    @pl.when(step == 0)
    def _():
        for c in page_copies(step, slot):
            c.start()

    @pl.when(step + 1 < pl.num_programs(0) * N_HEADS)
    def _():
        for c in page_copies(step + 1, 1 - slot):
            c.start()

    for c in page_copies(step, slot):
        c.wait()

    nt = (((1,), (1,)), ((), ()))
    k_own = ko_ref[0]
    row = lax.broadcasted_iota(jnp.int32, (dec_seq, 1), 0)
    for l in range(dec_seq):
        qs = q_ref[0, l:l + 1, :] * (HEAD_DIM ** -0.5)
        q8 = jnp.broadcast_to(qs, (8, HEAD_DIM)).astype(BF16)
        s_sel = [lax.dot_general(q8, kbuf[slot, l, p].astype(BF16), nt, preferred_element_type=F32)[0:1]
                 for p in range(n_pages)]
        s_own = jnp.sum(k_own * qs, axis=1, keepdims=True)
        s_own = jnp.where(row <= l, s_own, NEG_INF)
        m = jnp.max(s_own, axis=0, keepdims=True)
        for s in s_sel:
            m = jnp.maximum(m, jnp.max(s, axis=1, keepdims=True))
        p_own = jnp.exp(s_own - m)
        denom = jnp.sum(p_own, axis=0, keepdims=True)
        acc = jnp.sum(p_own * vo_ref[0], axis=0, keepdims=True)
        for p in range(n_pages):
            pr = jnp.exp(s_sel[p] - m)
            denom = denom + jnp.sum(pr, axis=1, keepdims=True)
            p8 = jnp.broadcast_to(pr, (8, PAGE_SIZE)).astype(BF16)
            acc = acc + _dot(p8, vbuf[slot, l, p].astype(BF16))[0:1]
        o_ref[0, l:l + 1, :] = acc / denom


def moba_sample_attention(q, k_new, v_new, cache_k, cache_v, phys, li):
    nb, dec_seq, _ = q.shape
    n_pages = phys.shape[-1]
    own_spec = pl.BlockSpec((1, dec_seq, HEAD_DIM), lambda b, h, ph: (b, 0, h))
    buf = pltpu.VMEM((2, dec_seq, n_pages, PAGE_SIZE, HEAD_DIM), F32)
    return pl.pallas_call(
        functools.partial(_sample_attn_kernel, n_pages=n_pages, dec_seq=dec_seq, li=li),
        grid_spec=pltpu.PrefetchScalarGridSpec(
            num_scalar_prefetch=1,
            grid=(nb, N_HEADS),
            in_specs=[own_spec, own_spec, own_spec,
                      pl.BlockSpec(memory_space=pl.ANY), pl.BlockSpec(memory_space=pl.ANY)],
            out_specs=own_spec,
            scratch_shapes=[buf, buf, pltpu.SemaphoreType.DMA((2, 2))],
        ),
        out_shape=jax.ShapeDtypeStruct((nb, dec_seq, D_MODEL), F32),
        compiler_params=_params(2),
        name="moba_sample_attention",
    )(phys.reshape(-1), q, k_new, v_new, cache_k, cache_v)


CONV_PAD_ROWS = 8


def _conv_gate_kernel(*refs, tl, has_halo):
    if has_halo:
        b_ref, c_ref, v_ref, ch_ref, vh_ref, hist_ref, w_ref, z_ref, st_ref, ext_ref = refs
    else:
        b_ref, c_ref, v_ref, hist_ref, w_ref, z_ref, st_ref, ext_ref = refs
    l_idx = pl.program_id(1)
    u = c_ref[0, 0] * v_ref[0, 0]
    ext_ref[CONV_PAD_ROWS:CONV_PAD_ROWS + tl, :] = u
    n_hist = CONV_WIDTH - 1

    @pl.when(l_idx == 0)
    def _():
        ext_ref[CONV_PAD_ROWS - n_hist:CONV_PAD_ROWS, :] = hist_ref[0]

    if has_halo:
        @pl.when(l_idx > 0)
        def _():
            ext_ref[0:CONV_PAD_ROWS, :] = ch_ref[0, 0] * vh_ref[0, 0]

    w = w_ref[...]
    conv = u * w[2:3, :]
    conv = conv + ext_ref[CONV_PAD_ROWS - 1:CONV_PAD_ROWS - 1 + tl, :] * w[1:2, :]
    conv = conv + ext_ref[CONV_PAD_ROWS - 2:CONV_PAD_ROWS - 2 + tl, :] * w[0:1, :]
    z_ref[0] = b_ref[0, 0] * conv

    @pl.when(l_idx == pl.num_programs(1) - 1)
    def _():
        st_ref[0] = ext_ref[CONV_PAD_ROWS + tl - n_hist:CONV_PAD_ROWS + tl, :]


def conv_gate(bcv, hist, w_conv, *, tl):
    _, nb, seq, _ = bcv.shape
    n_hist = CONV_WIDTH - 1
    nl = seq // tl
    has_halo = nl > 1
    part = lambda c: pl.BlockSpec((1, 1, tl, D_MODEL), lambda b, l: (c, b, l, 0))
    halo = lambda c: pl.BlockSpec(
        (1, 1, CONV_PAD_ROWS, D_MODEL),
        lambda b, l: (c, b, jnp.maximum(l * (tl // CONV_PAD_ROWS) - 1, 0), 0))
    in_specs = [part(0), part(1), part(2)]
    args = [bcv, bcv, bcv]
    if has_halo:
        in_specs += [halo(1), halo(2)]
        args += [bcv, bcv]
    in_specs += [pl.BlockSpec((1, n_hist, D_MODEL), lambda b, l: (b, 0, 0)),
                 pl.BlockSpec((CONV_WIDTH, D_MODEL), lambda b, l: (0, 0))]
    args += [hist, w_conv]
    return pl.pallas_call(
        functools.partial(_conv_gate_kernel, tl=tl, has_halo=has_halo),
        grid=(nb, nl),
        in_specs=in_specs,
        out_specs=[pl.BlockSpec((1, tl, D_MODEL), lambda b, l: (b, l, 0)),
                   pl.BlockSpec((1, n_hist, D_MODEL), lambda b, l: (b, 0, 0))],
        out_shape=[jax.ShapeDtypeStruct((nb, seq, D_MODEL), F32),
                   jax.ShapeDtypeStruct((nb, n_hist, D_MODEL), F32)],
        scratch_shapes=[pltpu.VMEM((CONV_PAD_ROWS + tl, D_MODEL), F32)],
        compiler_params=_params(2),
        name="conv_gate",
    )(*args)


def _layer_norm(z, g, b):
    mu = jnp.mean(z, axis=-1, keepdims=True)
    d = z - mu
    var = jnp.mean(d * d, axis=-1, keepdims=True)
    return d * lax.rsqrt(var + LN_EPS) * g + b


def _out_norm_kernel(a_ref, w_ref, x_ref, gate_ref, g_ref, b_ref, sc_ref, sh_ref, wr_ref,
                     xo_ref, h_ref, score_ref, acc_ref):
    k = pl.program_id(2)

    @pl.when(k == 0)
    def _():
        acc_ref[...] = jnp.zeros_like(acc_ref)

    acc_ref[...] += _dot(a_ref[0].astype(BF16), w_ref[...].astype(BF16))

    @pl.when(k == pl.num_programs(2) - 1)
    def _():
        xn = _layer_norm(DEEPNORM_ALPHA * x_ref[0] + gate_ref[0] * acc_ref[...], g_ref[...], b_ref[...])
        xo_ref[0] = xn
        h = xn * (1.0 + sc_ref[0]) + sh_ref[0]
        h_ref[0] = h
        logits = _dot3(h, wr_ref[...])
        score_ref[0] = 1.0 / (1.0 + jnp.exp(-logits))


def out_projection_norm(a, w, layer, x, gate, ln_g, ln_b, sc_next, sh_next, w_router, *, tm):
    nb, seq, kdim = a.shape
    r = gate.shape[1]
    tk = 512
    row = pl.BlockSpec((1, tm, D_MODEL), lambda b, m, k: (b, m, 0))
    mod = pl.BlockSpec((1, r, D_MODEL), lambda b, m, k: (b, 0, 0))
    vec = pl.BlockSpec((1, D_MODEL), lambda b, m, k: (0, 0))
    n_lanes = w_router.shape[1]
    return pl.pallas_call(
        _out_norm_kernel,
        grid=(nb, seq // tm, kdim // tk),
        in_specs=[
            pl.BlockSpec((1, tm, tk), lambda b, m, k: (b, m, k)),
            pl.BlockSpec((None, tk, D_MODEL), lambda b, m, k: (layer, k, 0)),
            row, mod, vec, vec, mod, mod,
            pl.BlockSpec((D_MODEL, n_lanes), lambda b, m, k: (0, 0)),
        ],
        out_specs=[row, row, pl.BlockSpec((1, tm, n_lanes), lambda b, m, k: (b, m, 0))],
        out_shape=[jax.ShapeDtypeStruct((nb, seq, D_MODEL), F32),
                   jax.ShapeDtypeStruct((nb, seq, D_MODEL), F32),
                   jax.ShapeDtypeStruct((nb, seq, n_lanes), F32)],
        scratch_shapes=[pltpu.VMEM((tm, D_MODEL), F32)],
        compiler_params=_params(3),
        name="out_projection_norm",
    )(a, w, x, gate, ln_g, ln_b, sc_next, sh_next, w_router)


def _swiglu(x_bf16, wg, wu, wd):
    g = _dot(x_bf16, wg)
    u = _dot(x_bf16, wu)
    return _dot((_silu(g) * u).astype(BF16), wd)


def _experts_kernel(be_ref, nb_ref, xs_ref, wg_ref, wu_ref, wd_ref, ys_ref, wgb, wub, wdb):
    i = pl.program_id(0)
    active = i < nb_ref[0]
    new_expert = (i == 0) | (be_ref[i] != be_ref[jnp.maximum(i - 1, 0)])

    @pl.when(active & new_expert)
    def _():
        wgb[...] = wg_ref[0].astype(BF16)
        wub[...] = wu_ref[0].astype(BF16)
        wdb[...] = wd_ref[0].astype(BF16)

    @pl.when(active)
    def _():
        ys_ref[...] = _swiglu(xs_ref[...].astype(BF16), wgb[...], wub[...], wdb[...])

    @pl.when(jnp.logical_not(active))
    def _():
        ys_ref[...] = jnp.zeros_like(ys_ref)


def routed_expert_rows(block_expert, n_used_blocks, xs, w_gate, w_up, w_down, layer):
    n_rows = xs.shape[0]
    bm = MOE_ROW_BLOCK
    wspec = lambda shape: pl.BlockSpec((None, 1) + shape, lambda i, be, nb: (layer, be[i], 0, 0))
    return pl.pallas_call(
        _experts_kernel,
        grid_spec=pltpu.PrefetchScalarGridSpec(
            num_scalar_prefetch=2,
            grid=(n_rows // bm,),
            in_specs=[
                pl.BlockSpec((bm, D_MODEL), lambda i, be, nb: (i, 0)),
                wspec((D_MODEL, D_EXPERT)), wspec((D_MODEL, D_EXPERT)), wspec((D_EXPERT, D_MODEL)),
            ],
            out_specs=pl.BlockSpec((bm, D_MODEL), lambda i, be, nb: (i, 0)),
            scratch_shapes=[pltpu.VMEM((D_MODEL, D_EXPERT), BF16),
                            pltpu.VMEM((D_MODEL, D_EXPERT), BF16),
                            pltpu.VMEM((D_EXPERT, D_MODEL), BF16)],
        ),
        out_shape=jax.ShapeDtypeStruct((n_rows, D_MODEL), F32),
        compiler_params=_params(1),
        name="routed_experts",
    )(block_expert, n_used_blocks, xs, w_gate, w_up, w_down)


def _shared_norm_kernel(h_ref, x_ref, y_ref, rg_ref, gate_ref, g_ref, b_ref, wg_ref, wu_ref, wd_ref,
                        xo_ref, wgb, wub, wdb):
    @pl.when((pl.program_id(0) == 0) & (pl.program_id(1) == 0))
    def _():
        wgb[...] = wg_ref[...].astype(BF16)
        wub[...] = wu_ref[...].astype(BF16)
        wdb[...] = wd_ref[...].astype(BF16)

    rg = rg_ref[...]
    routed = y_ref[0] * rg[:, 0:1]
    for k in range(1, TOP_K):
        routed = routed + y_ref[k] * rg[:, k:k + 1]
    f = routed + _swiglu(h_ref[0].astype(BF16), wgb[...], wub[...], wdb[...])
    xo_ref[0] = _layer_norm(DEEPNORM_ALPHA * x_ref[0] + gate_ref[0] * f, g_ref[...], b_ref[...])


def shared_expert_norm(h, x, y_tok, route_gate, tok0, gate, ln_g, ln_b, ws_gate, ws_up, ws_down, layer, *, tm):
    nb, seq, _ = h.shape
    r = gate.shape[1]
    d_shared = ws_gate.shape[2]
    assert tok0 % tm == 0 and seq % tm == 0
    tok_block = lambda b, m: tok0 // tm + b * (seq // tm) + m
    row = pl.BlockSpec((1, tm, D_MODEL), lambda b, m: (b, m, 0))
    vec = pl.BlockSpec((1, D_MODEL), lambda b, m: (0, 0))
    whole = lambda shape: pl.BlockSpec((None,) + shape, lambda b, m: (layer, 0, 0), pipeline_mode=pl.Buffered(1))
    return pl.pallas_call(
        _shared_norm_kernel,
        grid=(nb, seq // tm),
        in_specs=[row, row,
                  pl.BlockSpec((TOP_K, tm, D_MODEL), lambda b, m: (0, tok_block(b, m), 0)),
                  pl.BlockSpec((tm, TOP_K), lambda b, m: (tok_block(b, m), 0)),
                  pl.BlockSpec((1, r, D_MODEL), lambda b, m: (b, 0, 0)), vec, vec,
                  whole((D_MODEL, d_shared)), whole((D_MODEL, d_shared)), whole((d_shared, D_MODEL))],
        out_specs=row,
        out_shape=jax.ShapeDtypeStruct((nb, seq, D_MODEL), F32),
        scratch_shapes=[pltpu.VMEM((D_MODEL, d_shared), BF16),
                        pltpu.VMEM((D_MODEL, d_shared), BF16),
                        pltpu.VMEM((d_shared, D_MODEL), BF16)],
        compiler_params=_params(2),
        name="shared_expert_norm",
    )(h, x, y_tok, route_gate, gate, ln_g, ln_b, ws_gate, ws_up, ws_down)


ROUTE_TILE = 256
GROUP_SIZE = N_EXPERTS // N_GROUPS


def _first_argmax(x, ids, n):
    best = jnp.max(x, axis=0, keepdims=True)
    first = jnp.min(jnp.where(x == best, ids, float(n)), axis=0, keepdims=True)
    return best, first


def _route_kernel(s_ref, b_ref, idx_ref, gate_ref, rank_ref, cnt_ref, carry_ref, *, n_valid):
    i = pl.program_id(0)
    tm = ROUTE_TILE

    @pl.when(i == 0)
    def _():
        carry_ref[...] = jnp.zeros_like(carry_ref)

    raw = s_ref[...].T[:N_EXPERTS]
    biased = raw + b_ref[...]
    member = lax.broadcasted_iota(jnp.int32, (GROUP_SIZE, tm), 0).astype(F32)
    group_id = lax.broadcasted_iota(jnp.int32, (N_GROUPS, tm), 0).astype(F32)
    gs = jnp.zeros((N_GROUPS, tm), F32)
    for g in range(N_GROUPS):
        x = biased[g * GROUP_SIZE:(g + 1) * GROUP_SIZE]
        m1, i1 = _first_argmax(x, member, GROUP_SIZE)
        m2 = jnp.max(jnp.where(member == i1, NEG_INF, x), axis=0, keepdims=True)
        gs = jnp.where(group_id == g, m1 + m2, gs)
    keep = jnp.zeros((N_GROUPS, tm), F32)
    for _ in range(TOPK_GROUPS):
        _, first = _first_argmax(gs, group_id, N_GROUPS)
        pick = group_id == first
        keep = jnp.where(pick, 1.0, keep)
        gs = jnp.where(pick, NEG_INF, gs)
    masked = jnp.concatenate(
        [jnp.where(keep[g:g + 1] > 0.0, biased[g * GROUP_SIZE:(g + 1) * GROUP_SIZE], NEG_INF)
         for g in range(N_GROUPS)], axis=0)
    expert_id = lax.broadcasted_iota(jnp.int32, (N_EXPERTS, tm), 0).astype(F32)
    slot = lax.broadcasted_iota(jnp.int32, (TOP_K, tm), 0)
    idx = jnp.zeros((TOP_K, tm), F32)
    w = jnp.zeros((TOP_K, tm), F32)
    chosen = jnp.zeros((N_EXPERTS, tm), F32)
    picks = []
    for k in range(TOP_K):
        _, first = _first_argmax(masked, expert_id, N_EXPERTS)
        pick = expert_id == first
        picks.append(pick)
        idx = jnp.where(slot == k, first, idx)
        w = jnp.where(slot == k, jnp.sum(jnp.where(pick, raw, 0.0), axis=0, keepdims=True), w)
        chosen = jnp.where(pick, 1.0, chosen)
        masked = jnp.where(pick, NEG_INF, masked)
    gate_ref[...] = w / jnp.sum(w, axis=0, keepdims=True) * ROUTED_SCALE
    idx_ref[...] = idx.astype(jnp.int32)

    tok = lax.broadcasted_iota(jnp.int32, (1, tm), 1) + i * tm
    chosen = jnp.where(tok < n_valid, chosen, 0.0)
    earlier = (lax.broadcasted_iota(jnp.int32, (tm, tm), 0)
               < lax.broadcasted_iota(jnp.int32, (tm, tm), 1)).astype(BF16)
    before = _dot(chosen.astype(BF16), earlier) + carry_ref[...]
    rank = jnp.zeros((TOP_K, tm), F32)
    for k in range(TOP_K):
        rank = jnp.where(slot == k, jnp.sum(jnp.where(picks[k], before, 0.0), axis=0, keepdims=True), rank)
    rank_ref[...] = rank.astype(jnp.int32)
    carry_ref[...] += jnp.sum(chosen, axis=1, keepdims=True)

    @pl.when(i == pl.num_programs(0) - 1)
    def _():
        cnt_ref[...] = jnp.broadcast_to(carry_ref[...], cnt_ref.shape).astype(jnp.int32)


def route_tokens(scores, b_router, n_valid):
    n_pad = scores.shape[0]
    col = pl.BlockSpec((TOP_K, ROUTE_TILE), lambda i: (0, i))
    return pl.pallas_call(
        functools.partial(_route_kernel, n_valid=n_valid),
        grid=(n_pad // ROUTE_TILE,),
        in_specs=[pl.BlockSpec((ROUTE_TILE, scores.shape[1]), lambda i: (i, 0)),
                  pl.BlockSpec((N_EXPERTS, 1), lambda i: (0, 0))],
        out_specs=[col, col, col, pl.BlockSpec((N_EXPERTS, 128), lambda i: (0, 0))],
        out_shape=[jax.ShapeDtypeStruct((TOP_K, n_pad), jnp.int32),
                   jax.ShapeDtypeStruct((TOP_K, n_pad), F32),
                   jax.ShapeDtypeStruct((TOP_K, n_pad), jnp.int32),
                   jax.ShapeDtypeStruct((N_EXPERTS, 128), jnp.int32)],
        scratch_shapes=[pltpu.VMEM((N_EXPERTS, 1), F32)],
        compiler_params=_params(1),
        name="route_tokens",
    )(scores, b_router.reshape(N_EXPERTS, 1))


def _dispatch(idx_t, rank_t, counts, n_tok):
    bm = MOE_ROW_BLOCK
    n_blocks = -(-(n_tok * TOP_K) // bm) + N_EXPERTS
    padded = (counts + bm - 1) // bm * bm
    pad_end = jnp.cumsum(padded)
    pad_start = pad_end - padded
    expert_start = jnp.sum(jnp.where(idx_t[..., None] == jnp.arange(N_EXPERTS), pad_start, 0), axis=-1)
    dest = (expert_start + rank_t)[:, :n_tok]
    tok = jnp.broadcast_to(jnp.arange(n_tok, dtype=jnp.int32)[None], dest.shape)
    row_tok = jnp.zeros((n_blocks * bm,), jnp.int32).at[dest.reshape(-1)].set(
        tok.reshape(-1), unique_indices=True, mode='promise_in_bounds')
    block_start = jnp.arange(n_blocks, dtype=jnp.int32) * bm
    block_expert = jnp.minimum(
        jnp.sum((pad_end[None, :] <= block_start[:, None]).astype(jnp.int32), axis=1), N_EXPERTS - 1)
    n_used = (pad_end[-1] // bm).astype(jnp.int32).reshape(1)
    return dest, row_tok, block_expert, n_used


def _rope_tables(pos):
    half = HEAD_DIM // 2
    inv_freq = ROPE_THETA ** (-jnp.arange(half, dtype=F32) / half)
    ang = pos.astype(F32)[:, None] * inv_freq[None, :]
    cos, sin = jnp.cos(ang), jnp.sin(ang)
    return jnp.concatenate([cos, cos], axis=1), jnp.concatenate([-sin, sin], axis=1)


def kernel(x_prompt, x_sample, cache_k, cache_v, state_conv, page_table, c_prompt, c_sample, ada_w, ada_b, ln_g, ln_b, attn_w_qkv, attn_w_o, conv_w_in, conv_w, conv_w_out, router_w, router_b, exp_w_gate, exp_w_up, exp_w_down, shared_w_gate, shared_w_up, shared_w_down):
    n_p, seq, d = x_prompt.shape
    n_s, dec_seq, _ = x_sample.shape
    n_pages = page_table.shape[1]
    past_len = n_pages * PAGE_SIZE
    n_past_blocks = past_len // MOBA_BLOCK
    assert past_len % MOBA_BLOCK == 0 and (past_len + dec_seq - 1) // MOBA_BLOCK == n_past_blocks
    assert n_past_blocks >= MOBA_TOPK and n_p + n_s <= N_COND_ROWS and dec_seq <= 8
    t_p, t_s = n_p * seq, n_s * dec_seq
    t_all = t_p + t_s
    t_pad = -(-t_all // ROUTE_TILE) * ROUTE_TILE

    c_all = jnp.zeros((N_COND_ROWS, d), F32).at[:n_p].set(c_prompt).at[n_p:n_p + n_s].set(c_sample)
    mods = ada_modulation_all(c_all, ada_w.reshape(2 * DEPTH, d, 3 * d), ada_b.reshape(2 * DEPTH, 1, 3 * d))

    def mod_of(i, j):
        m = mods[2 * i + j].reshape(N_COND_ROWS, 3, d)
        m_p = m[:n_p][:, None]
        m_s = jnp.repeat(m[n_p:n_p + n_s], dec_seq, axis=0)[None]
        return [(m_p[:, :, c], m_s[:, :, c]) for c in range(3)]

    cos_p, sin_p = _rope_tables(jnp.arange(seq))
    cos_s, sin_s = _rope_tables(past_len + jnp.tile(jnp.arange(dec_seq), n_s))

    x_p = x_prompt
    x_s = x_sample.reshape(1, t_s, d)
    k_p_rows, v_p_rows, k_s_rows, v_s_rows, conv_p_rows, conv_s_rows = [], [], [], [], [], []
    for i in range(DEPTH):
        li = i // 2
        (sh_p, sh_s), (sc_p, sc_s), (g_p, g_s) = mod_of(i, 0)
        (sh2_p, sh2_s), (sc2_p, sc2_s), (g2_p, g2_s) = mod_of(i, 1)
        if i % 2 == 0:
            qkv_p = modulated_projection(x_p, sc_p, sh_p, attn_w_qkv, li, cos_p, sin_p, rope=True, tm=1024)
            qkv_s = modulated_projection(x_s, sc_s, sh_s, attn_w_qkv, li, cos_s, sin_s, rope=True, tm=t_s)
            a_p, k_p, v_p = moba_prompt_attention(qkv_p)
            q_s, k_s, v_s = (qkv_s[c].reshape(n_s, dec_seq, d) for c in range(3))
            kmean = paged_block_means(cache_k, page_table, li).reshape(n_s, n_past_blocks, d)
            q_pad = jnp.zeros((n_s, 8, d), F32).at[:, :dec_seq].set(q_s)
            sel = sample_block_select(q_pad, kmean)[:, :, :dec_seq, :MOBA_TOPK]
            sel = sel.transpose(0, 2, 1, 3)
            ppb = MOBA_BLOCK // PAGE_SIZE
            logical = (sel[..., None] * ppb + jnp.arange(ppb)).reshape(n_s, dec_seq, N_HEADS, MOBA_TOPK * ppb)
            phys = page_table[jnp.arange(n_s)[:, None, None, None], logical]
            a_s = moba_sample_attention(q_s, k_s, v_s, cache_k, cache_v, phys, li).reshape(1, t_s, d)
            k_p_rows.append(k_p)
            v_p_rows.append(v_p)
            k_s_rows.append(k_s)
            v_s_rows.append(v_s)
            w_out = attn_w_o
        else:
            bcv_p = modulated_projection(x_p, sc_p, sh_p, conv_w_in, li, cos_p, sin_p, rope=False, tm=1024)
            bcv_s = modulated_projection(x_s, sc_s, sh_s, conv_w_in, li, cos_s, sin_s, rope=False, tm=t_s)
            a_p, st_p = conv_gate(bcv_p, jnp.zeros((n_p, CONV_WIDTH - 1, d), F32), conv_w[li], tl=512)
            a_s, st_s = conv_gate(bcv_s.reshape(3, n_s, dec_seq, d), state_conv[:, li], conv_w[li], tl=dec_seq)
            a_s = a_s.reshape(1, t_s, d)
            conv_p_rows.append(st_p)
            conv_s_rows.append(st_s)
            w_out = conv_w_out
        g1, b1 = ln_g[i, 0][None], ln_b[i, 0][None]
        w_router = jnp.pad(router_w[i], ((0, 0), (0, 128 - N_EXPERTS)))
        x_p, h_p, score_p = out_projection_norm(a_p, w_out, li, x_p, g_p, g1, b1, sc2_p, sh2_p, w_router, tm=512)
        x_s, h_s, score_s = out_projection_norm(a_s, w_out, li, x_s, g_s, g1, b1, sc2_s, sh2_s, w_router, tm=t_s)

        h_all = jnp.concatenate([h_p.reshape(t_p, d), h_s.reshape(t_s, d)], axis=0)
        scores = jnp.concatenate([score_p.reshape(t_p, 128), score_s.reshape(t_s, 128),
                                  jnp.zeros((t_pad - t_all, 128), F32)], axis=0)
        idx_t, gate_t, rank_t, counts = route_tokens(scores, router_b[i], t_all)
        dest, row_tok, block_expert, n_used = _dispatch(idx_t, rank_t, counts[:, 0], t_all)
        ys = routed_expert_rows(block_expert, n_used, h_all[row_tok], exp_w_gate, exp_w_up, exp_w_down, i)
        y_tok = ys[dest]
        route_gate = gate_t[:, :t_all].T
        g2, b2 = ln_g[i, 1][None], ln_b[i, 1][None]
        x_p = shared_expert_norm(h_p, x_p, y_tok, route_gate, 0, g2_p, g2, b2,
                                 shared_w_gate, shared_w_up, shared_w_down, i, tm=128)
        x_s = shared_expert_norm(h_s, x_s, y_tok, route_gate, t_p, g2_s, g2, b2,
                                 shared_w_gate, shared_w_up, shared_w_down, i, tm=t_s)

    heads = lambda rows, n, l: jnp.stack(rows, axis=1).reshape(n, len(rows), l, N_HEADS, HEAD_DIM)
    return (x_p, x_s.reshape(n_s, dec_seq, d),
            jnp.concatenate(k_p_rows, axis=1), jnp.concatenate(v_p_rows, axis=1), jnp.stack(conv_p_rows, axis=1),
            heads(k_s_rows, n_s, dec_seq), heads(v_s_rows, n_s, dec_seq), jnp.stack(conv_s_rows, axis=1))
```
